```python
import math
import jax, jax.numpy as jnp
from jax import lax
import numpy as np

D_MODEL = 1024
BATCH = 16
SEQ = 2048
DEPTH = 4

HEAD_DIM = 64
ROT_DIM = HEAD_DIM // 4
ROPE_THETA = 500000.0
Q_BLOCK = 128
EPS = 1e-6

FOX_HEADS = D_MODEL // (2 * HEAD_DIM)
FOX_WIDTH = FOX_HEADS * HEAD_DIM
FOX_FGATE_BIAS = 3.0
DIFF_V_DIM = 2 * HEAD_DIM
DIFF_HEADS = D_MODEL // (2 * DIFF_V_DIM)
DIFF_WIDTH = DIFF_HEADS * DIFF_V_DIM
DSA_HEADS = D_MODEL // (2 * HEAD_DIM)
DSA_NOPE = HEAD_DIM - ROT_DIM
DSA_WIDTH = DSA_HEADS * HEAD_DIM
DSA_KV_RANK = 128
IDX_HEADS = 4
IDX_DIM = 64
IDX_TOPK_MAX = 256
SB_HEADS = D_MODEL // (2 * HEAD_DIM)
SB_WIDTH = SB_HEADS * HEAD_DIM
D_FF = 2816
CONV_WIDTH = 3

EVEN_SPLITS = (FOX_WIDTH, FOX_WIDTH, FOX_WIDTH, FOX_HEADS,
               DIFF_WIDTH, DIFF_WIDTH, DIFF_WIDTH)
ODD_SPLITS = (DSA_WIDTH, DSA_KV_RANK, ROT_DIM, IDX_HEADS * IDX_DIM, IDX_HEADS, IDX_DIM,
              SB_WIDTH, SB_WIDTH, SB_WIDTH)
EVEN_IN = sum(EVEN_SPLITS)
ODD_IN = sum(ODD_SPLITS)
N_EVEN = (DEPTH + 1) // 2
N_ODD = DEPTH // 2

kernel_name = "hybrid_fox_diff_dsa_stickbreak_convffn"


def rms_norm(x, g):
    xf = x.astype(jnp.float32)
    y = xf * lax.rsqrt(jnp.mean(xf * xf, axis=-1, keepdims=True) + EPS)
    return (y * g.astype(jnp.float32)).astype(x.dtype)


def split_cols(z, sizes):
    cuts = []
    acc = 0
    for s in sizes[:-1]:
        acc += s
        cuts.append(acc)
    return jnp.split(z, cuts, axis=-1)


def rope_tables(positions):
    inv_freq = ROPE_THETA ** (-jnp.arange(0, ROT_DIM, 2, dtype=jnp.float32) / ROT_DIM)
    ang = positions.astype(jnp.float32)[..., None] * inv_freq
    return jnp.cos(ang), jnp.sin(ang)


def apply_partial_rope(x, cos, sin):
    half = ROT_DIM // 2
    xr = x[..., :ROT_DIM].astype(jnp.float32)
    x1, x2 = xr[..., :half], xr[..., half:]
    rot = jnp.concatenate([x1 * cos - x2 * sin, x2 * cos + x1 * sin], axis=-1)
    return jnp.concatenate([rot.astype(x.dtype), x[..., ROT_DIM:]], axis=-1)


def sweep(block_fn, seq_len):
    out = lax.map(block_fn, jnp.arange(seq_len // Q_BLOCK))
    out = jnp.moveaxis(out, 0, 1)
    return out.reshape((out.shape[0], seq_len) + out.shape[3:])


def gather_rows(table, idx):
    return jax.vmap(lambda t, i: jnp.take(t, i, axis=0))(table, idx)


def fox_attention(q, k, v, log_f):
    seq_len = q.shape[1]
    scale = HEAD_DIM ** -0.5
    c = jnp.transpose(jnp.cumsum(log_f, axis=1), (0, 2, 1))
    kpos = jnp.arange(seq_len)

    def block(i):
        start = i * Q_BLOCK
        qpos = start + jnp.arange(Q_BLOCK)
        qb = lax.dynamic_slice_in_dim(q, start, Q_BLOCK, axis=1)
        cb = lax.dynamic_slice_in_dim(c, start, Q_BLOCK, axis=2)
        s = jnp.einsum('bqhd,bkhd->bhqk', qb, k).astype(jnp.float32) * scale
        s = s + (cb[..., :, None] - c[..., None, :])
        s = jnp.where(qpos[:, None] >= kpos[None, :], s, -jnp.inf)
        p = jax.nn.softmax(s, axis=-1).astype(v.dtype)
        return jnp.einsum('bhqk,bkhd->bqhd', p, v)

    return sweep(block, seq_len)


def diff_attention(q1, q2, k1, k2, v, lam):
    seq_len = q1.shape[1]
    scale = HEAD_DIM ** -0.5
    kpos = jnp.arange(seq_len)

    def block(i):
        start = i * Q_BLOCK
        qpos = start + jnp.arange(Q_BLOCK)
        mask = qpos[:, None] >= kpos[None, :]
        q1b = lax.dynamic_slice_in_dim(q1, start, Q_BLOCK, axis=1)
        q2b = lax.dynamic_slice_in_dim(q2, start, Q_BLOCK, axis=1)
        s1 = jnp.einsum('bqhd,bkhd->bhqk', q1b, k1).astype(jnp.float32) * scale
        s2 = jnp.einsum('bqhd,bkhd->bhqk', q2b, k2).astype(jnp.float32) * scale
        p1 = jax.nn.softmax(jnp.where(mask, s1, -jnp.inf), axis=-1)
        p2 = jax.nn.softmax(jnp.where(mask, s2, -jnp.inf), axis=-1)
        p = (p1 - lam * p2).astype(v.dtype)
        return jnp.einsum('bhqk,bkhd->bqhd', p, v)

    return sweep(block, seq_len)


def stick_breaking_attention(q, k, v):
    seq_len = q.shape[1]
    scale = HEAD_DIM ** -0.5
    kpos = jnp.arange(seq_len)

    def block(i):
        start = i * Q_BLOCK
        qpos = start + jnp.arange(Q_BLOCK)
        mask = kpos[None, :] < qpos[:, None]
        qb = lax.dynamic_slice_in_dim(q, start, Q_BLOCK, axis=1)
        z = jnp.einsum('bqhd,bkhd->bhqk', qb, k).astype(jnp.float32) * scale
        log1m = jnp.where(mask, jax.nn.log_sigmoid(-z), 0.0)
        after = lax.cumsum(log1m, axis=3, reverse=True) - log1m
        a = jnp.where(mask, jnp.exp(jax.nn.log_sigmoid(z) + after), 0.0)
        return jnp.einsum('bhqk,bkhd->bqhd', a.astype(v.dtype), v)

    return sweep(block, seq_len)


def dsa_attention(q_abs, q_rope, c_kv, k_rope, iq, iw, ik, top_k):
    seq_len = q_abs.shape[1]
    scale = HEAD_DIM ** -0.5
    kpos = jnp.arange(seq_len)

    def block(i):
        start = i * Q_BLOCK
        qpos = start + jnp.arange(Q_BLOCK)
        iqb = lax.dynamic_slice_in_dim(iq, start, Q_BLOCK, axis=1)
        iwb = lax.dynamic_slice_in_dim(iw, start, Q_BLOCK, axis=1)
        logits = jnp.einsum('bqhd,bkd->bqhk', iqb, ik).astype(jnp.float32)
        score = jnp.einsum('bqhk,bqh->bqk', jax.nn.relu(logits), iwb.astype(jnp.float32))
        score = jnp.where(kpos[None, None, :] <= qpos[None, :, None], score, -jnp.inf)
        _, sel = lax.top_k(score, top_k)
        valid = sel <= qpos[None, :, None]
        c_sel = gather_rows(c_kv, sel)
        kr_sel = gather_rows(k_rope, sel)
        qab = lax.dynamic_slice_in_dim(q_abs, start, Q_BLOCK, axis=1)
        qrb = lax.dynamic_slice_in_dim(q_rope, start, Q_BLOCK, axis=1)
        s = (jnp.einsum('bqhr,bqkr->bqhk', qab, c_sel)
             + jnp.einsum('bqhd,bqkd->bqhk', qrb, kr_sel)).astype(jnp.float32) * scale
        s = jnp.where(valid[:, :, None, :], s, -jnp.inf)
        p = jax.nn.softmax(s, axis=-1).astype(c_kv.dtype)
        return jnp.einsum('bqhk,bqkr->bqhr', p, c_sel)

    return sweep(block, seq_len)


def even_mixer(h, cos, sin, w_in, fgate_b, lq1, lk1, lq2, lk2, subln, w_out, layer_idx):
    b, s, _ = h.shape
    aq, ak, av, af, bq, bk, bv = split_cols(h @ w_in, EVEN_SPLITS)
    log_f = jax.nn.log_sigmoid((af + fgate_b).astype(jnp.float32))
    o_a = fox_attention(aq.reshape(b, s, FOX_HEADS, HEAD_DIM),
                        ak.reshape(b, s, FOX_HEADS, HEAD_DIM),
                        av.reshape(b, s, FOX_HEADS, HEAD_DIM), log_f)
    cs, sn = cos[:, :, None, :], sin[:, :, None, :]
    bq = bq.reshape(b, s, DIFF_HEADS, 2, HEAD_DIM)
    bk = bk.reshape(b, s, DIFF_HEADS, 2, HEAD_DIM)
    q1 = apply_partial_rope(bq[..., 0, :], cs, sn)
    q2 = apply_partial_rope(bq[..., 1, :], cs, sn)
    k1 = apply_partial_rope(bk[..., 0, :], cs, sn)
    k2 = apply_partial_rope(bk[..., 1, :], cs, sn)
    lam_init = 0.8 - 0.6 * math.exp(-0.3 * layer_idx)
    lam = (jnp.exp(jnp.sum(lq1.astype(jnp.float32) * lk1.astype(jnp.float32)))
           - jnp.exp(jnp.sum(lq2.astype(jnp.float32) * lk2.astype(jnp.float32))) + lam_init)
    o_b = diff_attention(q1, q2, k1, k2, bv.reshape(b, s, DIFF_HEADS, DIFF_V_DIM), lam)
    o_b = rms_norm(o_b, subln) * (1.0 - lam_init)
    o = jnp.concatenate([o_a.reshape(b, s, FOX_WIDTH), o_b.reshape(b, s, DIFF_WIDTH)], axis=-1)
    return o @ w_out


def odd_mixer(h, cos, sin, w_in, kv_norm, w_uk, w_uv, w_out, top_k):
    b, s, _ = h.shape
    cq, ckv, ckr, ciq, ciw, cik, sq, sk, sv = split_cols(h @ w_in, ODD_SPLITS)
    cs, sn = cos[:, :, None, :], sin[:, :, None, :]
    cq = apply_partial_rope(cq.reshape(b, s, DSA_HEADS, HEAD_DIM), cs, sn)
    q_rope, q_nope = cq[..., :ROT_DIM], cq[..., ROT_DIM:]
    c_kv = rms_norm(ckv, kv_norm)
    k_rope = apply_partial_rope(ckr, cos, sin)
    q_abs = jnp.einsum('bshn,rhn->bshr', q_nope, w_uk)
    iq = apply_partial_rope(ciq.reshape(b, s, IDX_HEADS, IDX_DIM), cs, sn)
    ik = apply_partial_rope(cik, cos, sin)
    iw = ciw * (IDX_HEADS ** -0.5)
    o_lat = dsa_attention(q_abs, q_rope, c_kv, k_rope, iq, iw, ik, top_k)
    o_c = jnp.einsum('bshr,rhd->bshd', o_lat, w_uv).reshape(b, s, DSA_WIDTH)
    o_d = stick_breaking_attention(sq.reshape(b, s, SB_HEADS, HEAD_DIM),
                                   sk.reshape(b, s, SB_HEADS, HEAD_DIM),
                                   sv.reshape(b, s, SB_HEADS, HEAD_DIM)).reshape(b, s, SB_WIDTH)
    return jnp.concatenate([o_c, o_d], axis=-1) @ w_out


def conv_ffn(h, w_in, conv_w, conv_b, w_out):
    u = h @ w_in
    u = lax.conv_general_dilated(u, conv_w[:, None, :], window_strides=(1,),
                                 padding=[(CONV_WIDTH - 1, 0)],
                                 dimension_numbers=('NWC', 'WIO', 'NWC'),
                                 feature_group_count=u.shape[-1]) + conv_b
    g, v = jnp.split(u, 2, axis=-1)
    return (jax.nn.silu(g) * v) @ w_out


def setup_inputs(seed: int = 0) -> dict:
    key = jax.random.key(seed)
    ks = jax.random.split(key, 24)
    f32 = jnp.float32

    def nrm(k, shape, scale):
        return jax.random.normal(k, shape, f32) * scale

    return {
        "x": nrm(ks[0], (BATCH, SEQ, D_MODEL), 1.0),
        "positions": jnp.broadcast_to(jnp.arange(SEQ, dtype=jnp.int32), (BATCH, SEQ)),
        "attn_norm": 1.0 + nrm(ks[1], (DEPTH, D_MODEL), 0.02),
        "ffn_norm": 1.0 + nrm(ks[2], (DEPTH, D_MODEL), 0.02),
        "final_norm": 1.0 + nrm(ks[3], (D_MODEL,), 0.02),
        "ev_w_in": nrm(ks[4], (N_EVEN, D_MODEL, EVEN_IN), D_MODEL ** -0.5),
        "ev_fgate_b": FOX_FGATE_BIAS + nrm(ks[5], (N_EVEN, FOX_HEADS), 0.1),
        "ev_lambda_q1": nrm(ks[6], (N_EVEN, HEAD_DIM), 0.1),
        "ev_lambda_k1": nrm(ks[7], (N_EVEN, HEAD_DIM), 0.1),
        "ev_lambda_q2": nrm(ks[8], (N_EVEN, HEAD_DIM), 0.1),
        "ev_lambda_k2": nrm(ks[9], (N_EVEN, HEAD_DIM), 0.1),
        "ev_subln": 1.0 + nrm(ks[10], (N_EVEN, DIFF_V_DIM), 0.02),
        "ev_w_out": nrm(ks[11], (N_EVEN, D_MODEL, D_MODEL), D_MODEL ** -0.5),
        "od_w_in": nrm(ks[12], (N_ODD, D_MODEL, ODD_IN), D_MODEL ** -0.5),
        "od_kv_norm": 1.0 + nrm(ks[13], (N_ODD, DSA_KV_RANK), 0.02),
        "od_w_uk": nrm(ks[14], (N_ODD, DSA_KV_RANK, DSA_HEADS, DSA_NOPE), DSA_KV_RANK ** -0.5),
        "od_w_uv": nrm(ks[15], (N_ODD, DSA_KV_RANK, DSA_HEADS, HEAD_DIM), DSA_KV_RANK ** -0.5),
        "od_w_out": nrm(ks[16], (N_ODD, D_MODEL, D_MODEL), D_MODEL ** -0.5),
        "ffn_w_in": nrm(ks[17], (DEPTH, D_MODEL, 2 * D_FF), D_MODEL ** -0.5),
        "ffn_conv_w": nrm(ks[18], (DEPTH, CONV_WIDTH, 2 * D_FF), CONV_WIDTH ** -0.5),
        "ffn_conv_b": nrm(ks[19], (DEPTH, 2 * D_FF), 0.02),
        "ffn_w_out": nrm(ks[20], (DEPTH, D_FF, D_MODEL), D_FF ** -0.5),
    }


def reference(x, positions, attn_norm, ffn_norm, final_norm,
              ev_w_in, ev_fgate_b, ev_lambda_q1, ev_lambda_k1, ev_lambda_q2, ev_lambda_k2,
              ev_subln, ev_w_out,
              od_w_in, od_kv_norm, od_w_uk, od_w_uv, od_w_out,
              ffn_w_in, ffn_conv_w, ffn_conv_b, ffn_w_out):
    cos, sin = rope_tables(positions)
    top_k = min(IDX_TOPK_MAX, x.shape[1] // 4)
    for layer in range(DEPTH):
        j = layer // 2
        h = rms_norm(x, attn_norm[layer])
        if layer % 2 == 0:
            x = x + even_mixer(h, cos, sin, ev_w_in[j], ev_fgate_b[j],
                               ev_lambda_q1[j], ev_lambda_k1[j], ev_lambda_q2[j], ev_lambda_k2[j],
                               ev_subln[j], ev_w_out[j], layer)
        else:
            x = x + odd_mixer(h, cos, sin, od_w_in[j], od_kv_norm[j], od_w_uk[j], od_w_uv[j],
                              od_w_out[j], top_k)
        h = rms_norm(x, ffn_norm[layer])
        x = x + conv_ffn(h, ffn_w_in[layer], ffn_conv_w[layer], ffn_conv_b[layer], ffn_w_out[layer])
    return rms_norm(x, final_norm)
```

```python
import functools
import math

import jax
import jax.numpy as jnp
from jax import lax
from jax.experimental import pallas as pl
from jax.experimental.pallas import tpu as pltpu

F32 = jnp.float32
BF16 = jnp.bfloat16
I32 = jnp.int32

D_MODEL = 1024
HEAD_DIM = 64
ROT_DIM = 16
ROT_HALF = 8
ROPE_THETA = 500000.0
EPS = 1e-6
LANES = 128
QK_SCALE = HEAD_DIM ** -0.5
FOX_WIDTH = 512
DIFF_WIDTH = 512
DIFF_HEADS = 4
DSA_WIDTH = 512
DSA_HEADS = 8
DSA_NOPE = HEAD_DIM - ROT_DIM
DSA_KV_RANK = 128
IDX_HEADS = 4
IDX_DIM = 64
IDX_TOPK_MAX = 256
SB_WIDTH = 512
D_FF = 2816
INT_MIN = -(2 ** 31)

VMEM_LIMIT = 56 * 1024 * 1024

PROJ_TM = 512
PROJ_CHUNK = 256
FFN_TM = 512
FFN_HALO = 16
FFN_CHUNK = 256
ATT_TQ = 256
ATT_TK = 256
DSA_TQ = 128
DSA_CLASSES = 4


def _cparams(sem):
    return pltpu.CompilerParams(dimension_semantics=sem, vmem_limit_bytes=VMEM_LIMIT)


def _rms_rows(x, g):
    ms = jnp.mean(x * x, axis=-1, keepdims=True)
    return x * lax.rsqrt(ms + EPS) * g


def _rope128(z, c, sa, sb):
    return z * c + pltpu.roll(z, LANES - ROT_HALF, axis=1) * sa + pltpu.roll(z, ROT_HALF, axis=1) * sb


def _in_proj_body(groups, x_ref, g_ref, w_ref, c_ref, sa_ref, sb_ref, kvg_ref, *rest):
    out_refs, h_scr = rest[:-1], rest[-1]
    h_scr[...] = _rms_rows(x_ref[...], g_ref[...]).astype(BF16)
    col = 0
    for (width, kind), o_ref in zip(groups, out_refs):
        cw = min(width, PROJ_CHUNK)
        for c0 in range(0, width, cw):
            zc = jnp.dot(h_scr[...], w_ref[:, col + c0:col + c0 + cw], preferred_element_type=F32)
            for l0 in range(0, cw, LANES):
                z = zc[:, l0:l0 + LANES]
                if kind == "rope":
                    z = _rope128(z, c_ref[...], sa_ref[...], sb_ref[...])
                elif kind == "kvnorm":
                    z = _rms_rows(z, kvg_ref[...])
                o_ref[:, c0 + l0:c0 + l0 + LANES] = z.astype(o_ref.dtype)
        col += width


def _in_proj(x, gain, w_cat, tabs, kv_gain, groups):
    t = x.shape[0]
    n = w_cat.shape[1]
    tm = PROJ_TM
    row = lambda i: (i, 0)
    const = lambda i: (0, 0)
    out_shape = [jax.ShapeDtypeStruct((t, w), F32 if kind == "f32" else BF16) for w, kind in groups]
    out_specs = [pl.BlockSpec((tm, w), row) for w, _ in groups]
    return pl.pallas_call(
        functools.partial(_in_proj_body, groups),
        grid=(t // tm,),
        in_specs=[pl.BlockSpec((tm, D_MODEL), row),
                  pl.BlockSpec((1, D_MODEL), const),
                  pl.BlockSpec((D_MODEL, n), const),
                  pl.BlockSpec((tm, LANES), row),
                  pl.BlockSpec((tm, LANES), row),
                  pl.BlockSpec((tm, LANES), row),
                  pl.BlockSpec((1, LANES), const)],
        out_specs=out_specs,
        out_shape=out_shape,
        scratch_shapes=[pltpu.VMEM((tm, D_MODEL), BF16)],
        compiler_params=_cparams(("parallel",)),
        name="in_proj",
    )(x, gain.reshape(1, D_MODEL), w_cat, tabs[0], tabs[1], tabs[2], kv_gain.reshape(1, LANES))


def _out_proj_body(x_ref, a_ref, b_ref, w_ref, o_ref):
    half = a_ref.shape[1]
    y = jnp.dot(a_ref[...], w_ref[:half, :], preferred_element_type=F32)
    y = y + jnp.dot(b_ref[...], w_ref[half:, :], preferred_element_type=F32)
    o_ref[...] = x_ref[...] + y


def _out_proj(x, a, b, w):
    t = x.shape[0]
    tm = PROJ_TM
    row = lambda i: (i, 0)
    const = lambda i: (0, 0)
    return pl.pallas_call(
        _out_proj_body,
        grid=(t // tm,),
        in_specs=[pl.BlockSpec((tm, D_MODEL), row),
                  pl.BlockSpec((tm, a.shape[1]), row),
                  pl.BlockSpec((tm, b.shape[1]), row),
                  pl.BlockSpec((D_MODEL, D_MODEL), const)],
        out_specs=pl.BlockSpec((tm, D_MODEL), row),
        out_shape=jax.ShapeDtypeStruct((t, D_MODEL), F32),
        compiler_params=_cparams(("parallel",)),
        name="out_proj",
    )(x, a, b, w)


def _ffn_body(tiles_per_seq, final, x_ref, xh_ref, g_ref, win_ref, cw_ref, cb_ref, wout_ref, fg_ref,
              o_ref, h_scr, acc_scr):
    tm = x_ref.shape[0]
    x = x_ref[...]
    g = g_ref[...]
    keep = (pl.program_id(0) % tiles_per_seq != 0).astype(F32)
    h_scr[:FFN_HALO, :] = (_rms_rows(xh_ref[...], g) * keep).astype(BF16)
    h_scr[FFN_HALO:, :] = _rms_rows(x, g).astype(BF16)

    def conv(u, c0):
        y = (pltpu.roll(u, 2, axis=0) * cw_ref[0:1, c0:c0 + FFN_CHUNK]
             + pltpu.roll(u, 1, axis=0) * cw_ref[1:2, c0:c0 + FFN_CHUNK]
             + u * cw_ref[2:3, c0:c0 + FFN_CHUNK])
        return y[FFN_HALO:, :] + cb_ref[0:1, c0:c0 + FFN_CHUNK]

    for ci in range(D_FF // FFN_CHUNK):
        cg = ci * FFN_CHUNK
        cv = D_FF + cg
        ug = jnp.dot(h_scr[...], win_ref[:, cg:cg + FFN_CHUNK], preferred_element_type=F32)
        uv = jnp.dot(h_scr[...], win_ref[:, cv:cv + FFN_CHUNK], preferred_element_type=F32)
        gate = conv(ug, cg)
        val = conv(uv, cv)
        act = (gate * jax.nn.sigmoid(gate) * val).astype(BF16)
        part = jnp.dot(act, wout_ref[cg:cg + FFN_CHUNK, :], preferred_element_type=F32)
        if ci == 0:
            acc_scr[...] = part
        else:
            acc_scr[...] += part
    y = x + acc_scr[...]
    if final:
        y = _rms_rows(y, fg_ref[...])
    o_ref[...] = y


def _ffn(x, gain, w_in, conv_w, conv_b, w_out, final_gain, seq, final):
    t = x.shape[0]
    tm = FFN_TM
    halo_blocks = tm // FFN_HALO
    row = lambda i: (i, 0)
    const = lambda i: (0, 0)
    return pl.pallas_call(
        functools.partial(_ffn_body, seq // tm, final),
        grid=(t // tm,),
        in_specs=[pl.BlockSpec((tm, D_MODEL), row),
                  pl.BlockSpec((FFN_HALO, D_MODEL), lambda i: (jnp.maximum(i * halo_blocks - 1, 0), 0)),
                  pl.BlockSpec((1, D_MODEL), const),
                  pl.BlockSpec((D_MODEL, 2 * D_FF), const),
                  pl.BlockSpec((3, 2 * D_FF), const),
                  pl.BlockSpec((1, 2 * D_FF), const),
                  pl.BlockSpec((D_FF, D_MODEL), const),
                  pl.BlockSpec((1, D_MODEL), const)],
        out_specs=pl.BlockSpec((tm, D_MODEL), row),
        out_shape=jax.ShapeDtypeStruct((t, D_MODEL), F32),
        scratch_shapes=[pltpu.VMEM((tm + FFN_HALO, D_MODEL), BF16), pltpu.VMEM((tm, D_MODEL), F32)],
        compiler_params=_cparams(("parallel",)),
        name="conv_ffn",
    )(x, x, gain.reshape(1, D_MODEL), w_in, conv_w, conv_b.reshape(1, 2 * D_FF), w_out,
      final_gain.reshape(1, D_MODEL))


def _split3(x):
    hi = x.astype(BF16)
    r = x - hi.astype(F32)
    mid = r.astype(BF16)
    lo = (r - mid.astype(F32)).astype(BF16)
    return hi, mid, lo


def _fgate_body(af_ref, b_ref, c_ref):
    s = af_ref.shape[0]
    blk = 512
    ri = lax.broadcasted_iota(I32, (blk, blk), 0)
    ci = lax.broadcasted_iota(I32, (blk, blk), 1)
    tri = jnp.where(ci <= ri, 1.0, 0.0).astype(BF16)
    carry = jnp.zeros((1, LANES), F32)
    for r0 in range(0, s, blk):
        lf = jax.nn.log_sigmoid(af_ref[r0:r0 + blk, :] + b_ref[...])
        hi, mid, lo = _split3(lf)
        c = (jnp.dot(tri, hi, preferred_element_type=F32)
             + jnp.dot(tri, mid, preferred_element_type=F32)
             + jnp.dot(tri, lo, preferred_element_type=F32)) + carry
        c_ref[r0:r0 + blk, :] = c
        carry = c[blk - 1:blk, :]


def _fgate_cumsum(af, bias, batch, seq):
    return pl.pallas_call(
        _fgate_body,
        grid=(batch,),
        in_specs=[pl.BlockSpec((seq, LANES), lambda b: (b, 0)),
                  pl.BlockSpec((1, LANES), lambda b: (0, 0))],
        out_specs=pl.BlockSpec((seq, LANES), lambda b: (b, 0)),
        out_shape=jax.ShapeDtypeStruct((batch * seq, LANES), F32),
        compiler_params=_cparams(("parallel",)),
        name="fox_gate_cumsum",
    )(af, bias)


def _nt_dot(a, b):
    return lax.dot_general(a, b, (((1,), (1,)), ((), ())), preferred_element_type=F32)


def _head_mask(hh):
    lane = lax.broadcasted_iota(I32, (1, LANES), 1)
    return (lane >= hh * HEAD_DIM) & (lane < (hh + 1) * HEAD_DIM)


def _causal_tile_mask(tq, tk, strict):
    qpos = lax.broadcasted_iota(I32, (tq, tk), 0)
    kpos = lax.broadcasted_iota(I32, (tq, tk), 1)
    return kpos < qpos if strict else kpos <= qpos


def _softmax_step(s, vt, carry):
    m, l, acc = carry
    m_new = jnp.maximum(m, jnp.max(s, axis=-1, keepdims=True))
    p = jnp.exp(s - m_new)
    alpha = jnp.exp(m - m_new)
    l = alpha * l + jnp.sum(p, axis=-1, keepdims=True)
    acc = alpha * acc + jnp.dot(p.astype(BF16), vt, preferred_element_type=F32)
    return m_new, l, acc


def _softmax_init(tq):
    return (jnp.full((tq, 1), -jnp.inf, F32), jnp.zeros((tq, 1), F32), jnp.zeros((tq, LANES), F32))


def _fox_body(q_ref, k_ref, v_ref, ccol_ref, crow_ref, o_ref):
    tq, tk = ATT_TQ, ATT_TK
    qi = pl.program_id(2)
    q = q_ref[...]
    out = jnp.zeros((tq, LANES), F32)
    for hh in range(2):
        hm = _head_mask(hh)
        qh = jnp.where(hm, q, jnp.zeros_like(q))
        cq = ccol_ref[:, hh:hh + 1]

        def tile(j, carry, diag):
            k0 = pl.multiple_of(j * tk, tk)
            kt = k_ref[pl.ds(k0, tk), :]
            vt = v_ref[pl.ds(k0, tk), :]
            vt = jnp.where(hm, vt, jnp.zeros_like(vt))
            s = _nt_dot(qh, kt) + (cq - crow_ref[j][hh:hh + 1, :])
            if diag:
                s = jnp.where(_causal_tile_mask(tq, tk, False), s, -jnp.inf)
            return _softmax_step(s, vt, carry)

        carry = lax.fori_loop(0, qi, lambda j, c: tile(j, c, False), _softmax_init(tq))
        _, l, acc = tile(qi, carry, True)
        out = out + acc / l
    o_ref[...] = out.astype(o_ref.dtype)


def _fox_attention(q, k, v, ccol, crow, batch, seq):
    tq = ATT_TQ
    nq = seq // tq
    pairs = q.shape[1] // LANES
    qmap = lambda b, p, i: (b * nq + i, p)
    kvmap = lambda b, p, i: (b, p)
    return pl.pallas_call(
        _fox_body,
        grid=(batch, pairs, nq),
        in_specs=[pl.BlockSpec((tq, LANES), qmap),
                  pl.BlockSpec((seq, LANES), kvmap),
                  pl.BlockSpec((seq, LANES), kvmap),
                  pl.BlockSpec((None, None, tq, 2), lambda b, p, i: (b, p, i, 0)),
                  pl.BlockSpec((None, None, seq // ATT_TK, 2, ATT_TK), lambda b, p, i: (b, p, 0, 0, 0))],
        out_specs=pl.BlockSpec((tq, LANES), qmap),
        out_shape=jax.ShapeDtypeStruct(q.shape, BF16),
        compiler_params=_cparams(("parallel", "parallel", "arbitrary")),
        name="fox_attention",
    )(q, k, v, ccol, crow)


def _diff_body(lam_init, q_ref, k_ref, v_ref, lq1_ref, lk1_ref, lq2_ref, lk2_ref, sub_ref, o_ref):
    tq, tk = ATT_TQ, ATT_TK
    qi = pl.program_id(2)
    q = q_ref[...]
    halves = []
    for hh in range(2):
        qh = jnp.where(_head_mask(hh), q, jnp.zeros_like(q))

        def tile(j, carry, diag):
            k0 = pl.multiple_of(j * tk, tk)
            s = _nt_dot(qh, k_ref[pl.ds(k0, tk), :])
            if diag:
                s = jnp.where(_causal_tile_mask(tq, tk, False), s, -jnp.inf)
            return _softmax_step(s, v_ref[pl.ds(k0, tk), :], carry)

        carry = lax.fori_loop(0, qi, lambda j, c: tile(j, c, False), _softmax_init(tq))
        _, l, acc = tile(qi, carry, True)
        halves.append(acc / l)
    lam = (jnp.exp(jnp.sum(lq1_ref[...] * lk1_ref[...], axis=-1, keepdims=True))
           - jnp.exp(jnp.sum(lq2_ref[...] * lk2_ref[...], axis=-1, keepdims=True)) + lam_init)
    o = halves[0] - lam * halves[1]
    o_ref[...] = (_rms_rows(o, sub_ref[...]) * (1.0 - lam_init)).astype(o_ref.dtype)


def _diff_attention(q, k, v, lq1, lk1, lq2, lk2, subln, lam_init, batch, seq):
    tq = ATT_TQ
    nq = seq // tq
    heads = q.shape[1] // LANES
    qmap = lambda b, h, i: (b * nq + i, h)
    kvmap = lambda b, h, i: (b, h)
    vec = lambda n: pl.BlockSpec((1, n), lambda b, h, i: (0, 0))
    return pl.pallas_call(
        functools.partial(_diff_body, lam_init),
        grid=(batch, heads, nq),
        in_specs=[pl.BlockSpec((tq, LANES), qmap),
                  pl.BlockSpec((seq, LANES), kvmap),
                  pl.BlockSpec((seq, LANES), kvmap),
                  vec(HEAD_DIM), vec(HEAD_DIM), vec(HEAD_DIM), vec(HEAD_DIM), vec(LANES)],
        out_specs=pl.BlockSpec((tq, LANES), qmap),
        out_shape=jax.ShapeDtypeStruct(q.shape, BF16),
        compiler_params=_cparams(("parallel", "parallel", "arbitrary")),
        name="diff_attention",
    )(q, k, v, lq1.reshape(1, -1), lk1.reshape(1, -1), lq2.reshape(1, -1), lk2.reshape(1, -1),
      subln.reshape(1, -1))


def _sb_body(q_ref, k_ref, v_ref, o_ref):
    tq, tk = ATT_TQ, ATT_TK
    qi = pl.program_id(2)
    q = q_ref[...]
    ri = lax.broadcasted_iota(I32, (tk, tk), 0)
    ci = lax.broadcasted_iota(I32, (tk, tk), 1)
    later = jnp.where(ri > ci, 1.0, 0.0).astype(BF16)
    out = jnp.zeros((tq, LANES), F32)
    for hh in range(2):
        hm = _head_mask(hh)
        qh = jnp.where(hm, q, jnp.zeros_like(q))

        def tile(j, carry, diag):
            run, acc = carry
            k0 = pl.multiple_of(j * tk, tk)
            vt = v_ref[pl.ds(k0, tk), :]
            vt = jnp.where(hm, vt, jnp.zeros_like(vt))
            z = _nt_dot(qh, k_ref[pl.ds(k0, tk), :])
            log1m = -(jnp.maximum(z, 0.0) + jnp.log1p(jnp.exp(-jnp.abs(z))))
            if diag:
                mask = _causal_tile_mask(tq, tk, True)
                log1m = jnp.where(mask, log1m, 0.0)
            hi = log1m.astype(BF16)
            lo = (log1m - hi.astype(F32)).astype(BF16)
            after = (jnp.dot(hi, later, preferred_element_type=F32)
                     + jnp.dot(lo, later, preferred_element_type=F32)) + run
            a = jnp.exp(z + log1m + after)
            if diag:
                a = jnp.where(mask, a, 0.0)
            acc = acc + jnp.dot(a.astype(BF16), vt, preferred_element_type=F32)
            run = run + jnp.sum(log1m, axis=-1, keepdims=True)
            return run, acc

        carry = tile(qi, (jnp.zeros((tq, 1), F32), jnp.zeros((tq, LANES), F32)), True)
        _, acc = lax.fori_loop(0, qi, lambda jj, c: tile(qi - 1 - jj, c, False), carry)
        out = out + acc
    o_ref[...] = out.astype(o_ref.dtype)


def _sb_attention(q, k, v, batch, seq):
    tq = ATT_TQ
    nq = seq // tq
    pairs = q.shape[1] // LANES
    qmap = lambda b, p, i: (b * nq + i, p)
    kvmap = lambda b, p, i: (b, p)
    return pl.pallas_call(
        _sb_body,
        grid=(batch, pairs, nq),
        in_specs=[pl.BlockSpec((tq, LANES), qmap),
                  pl.BlockSpec((seq, LANES), kvmap),
                  pl.BlockSpec((seq, LANES), kvmap)],
        out_specs=pl.BlockSpec((tq, LANES), qmap),
        out_shape=jax.ShapeDtypeStruct(q.shape, BF16),
        compiler_params=_cparams(("parallel", "parallel", "arbitrary")),
        name="stickbreak_attention",
    )(q, k, v)


def _dsa_block(nk, top_k, q0, iq_ref, iw_ref, cq_ref, ika_ref, ikb_ref, ckv_ref, kr_ref, mq_ref, wuv_ref,
               o_ref, key_scr, sel_scr):
    tq = DSA_TQ
    qpos = q0 + lax.broadcasted_iota(I32, (tq, nk), 0)
    kpos = lax.broadcasted_iota(I32, (tq, nk), 1)
    causal = kpos <= qpos

    score = jnp.zeros((tq, nk), F32)
    for p in range(IDX_HEADS // 2):
        iq2 = iq_ref[:, p * LANES:(p + 1) * LANES]
        for hh, ik_ref in enumerate((ika_ref, ikb_ref)):
            h = 2 * p + hh
            logits = _nt_dot(iq2, ik_ref[0:nk, :])
            score = score + jnp.maximum(logits, 0.0) * iw_ref[:, h:h + 1]

    bits = pltpu.bitcast(score, I32)
    key = bits ^ ((bits >> 31) & jnp.int32(0x7FFFFFFF))
    key = jnp.where(bits == jnp.int32(INT_MIN), 0, key)
    key_scr[:, 0:nk] = jnp.where(causal, key, jnp.int32(INT_MIN))

    def count(pred):
        return jnp.sum(jnp.where(pred, 1.0, 0.0), axis=-1, keepdims=True)

    def value_bit(i, tu):
        cand = tu | lax.shift_left(jnp.int32(1), 31 - i)
        cnt = count(key_scr[:, 0:nk] >= (cand ^ jnp.int32(INT_MIN)))
        return jnp.where(cnt >= top_k, cand, tu)

    thr = lax.fori_loop(0, 32, value_bit, jnp.zeros((tq, 1), I32)) ^ jnp.int32(INT_MIN)
    keys = key_scr[:, 0:nk]
    above = keys > thr
    tied = keys == thr
    need = top_k - count(above)

    index_bits = (nk - 1).bit_length()

    def index_bit(i, x):
        cand = x | lax.shift_left(jnp.int32(1), index_bits - 1 - i)
        cnt = count((key_scr[:, 0:nk] == thr) & (kpos < cand))
        return jnp.where(cnt < need, cand, x)

    cut = lax.fori_loop(0, index_bits, index_bit, jnp.zeros((tq, 1), I32))
    selected = (above | (tied & (kpos <= cut))) & causal
    sel_scr[:, 0:nk] = jnp.where(selected, 0.0, -jnp.inf)

    kv = jnp.concatenate([ckv_ref[0:nk, :], kr_ref[0:nk, :]], axis=1)
    for p in range(DSA_HEADS // 2):
        cq2 = cq_ref[:, p * LANES:(p + 1) * LANES]
        o_pair = jnp.zeros((tq, LANES), F32)
        for hh in range(2):
            h = 2 * p + hh
            qcat = jnp.dot(cq2, mq_ref[h], preferred_element_type=F32).astype(BF16)
            s = _nt_dot(qcat, kv) + sel_scr[:, 0:nk]
            m = jnp.max(s, axis=-1, keepdims=True)
            e = jnp.exp(s - m)
            l = jnp.sum(e, axis=-1, keepdims=True)
            p_att = (e / l).astype(BF16)
            o_lat = jnp.dot(p_att, ckv_ref[0:nk, :], preferred_element_type=F32).astype(BF16)
            o_pair = o_pair + jnp.dot(o_lat, wuv_ref[h], preferred_element_type=F32)
        o_ref[:, p * LANES:(p + 1) * LANES] = o_pair.astype(o_ref.dtype)


def _dsa_body(seq, top_k, *refs):
    qi = pl.program_id(1)
    per_class = (seq // DSA_TQ) // DSA_CLASSES
    for c in range(DSA_CLASSES):
        @pl.when(qi // per_class == c)
        def _():
            _dsa_block((c + 1) * per_class * DSA_TQ, top_k, qi * DSA_TQ, *refs)


def _dsa_attention(iq, iw, cq, ika, ikb, ckv, kr, mq, wuv, top_k, batch, seq):
    tq = DSA_TQ
    nq = seq // tq
    qmap = lambda b, i: (b * nq + i, 0)
    kmap = lambda b, i: (b, 0)
    cmap = lambda b, i: (0, 0, 0)
    return pl.pallas_call(
        functools.partial(_dsa_body, seq, top_k),
        grid=(batch, nq),
        in_specs=[pl.BlockSpec((tq, iq.shape[1]), qmap),
                  pl.BlockSpec((tq, LANES), qmap),
                  pl.BlockSpec((tq, DSA_WIDTH), qmap),
                  pl.BlockSpec((seq, LANES), kmap),
                  pl.BlockSpec((seq, LANES), kmap),
                  pl.BlockSpec((seq, LANES), kmap),
                  pl.BlockSpec((seq, LANES), kmap),
                  pl.BlockSpec(mq.shape, cmap),
                  pl.BlockSpec(wuv.shape, cmap)],
        out_specs=pl.BlockSpec((tq, DSA_WIDTH), qmap),
        out_shape=jax.ShapeDtypeStruct(cq.shape, BF16),
        scratch_shapes=[pltpu.VMEM((tq, seq), I32), pltpu.VMEM((tq, seq), F32)],
        compiler_params=_cparams(("parallel", "arbitrary")),
        name="dsa_attention",
    )(iq, iw, cq, ika, ikb, ckv, kr, mq, wuv)


def _rope_tables(positions):
    inv_freq = ROPE_THETA ** (-jnp.arange(0, ROT_DIM, 2, dtype=F32) / ROT_DIM)
    ang = positions.astype(F32).reshape(-1, 1) * inv_freq
    cos, sin = jnp.cos(ang), jnp.sin(ang)
    t = ang.shape[0]
    pad = jnp.zeros((t, HEAD_DIM - ROT_DIM), F32)
    zero8 = jnp.zeros((t, ROT_HALF), F32)
    c64 = jnp.concatenate([cos, cos, pad + 1.0], axis=1)
    sa64 = jnp.concatenate([-sin, zero8, pad], axis=1)
    sb64 = jnp.concatenate([zero8, sin, pad], axis=1)
    return tuple(jnp.tile(a, (1, LANES // HEAD_DIM)) for a in (c64, sa64, sb64))


def _pad_cols(w, width):
    return jnp.pad(w, ((0, 0), (0, width - w.shape[1])))


EVEN_GROUPS = ((FOX_WIDTH, "plain"), (FOX_WIDTH, "plain"), (FOX_WIDTH, "plain"),
               (DIFF_WIDTH, "rope"), (DIFF_WIDTH, "rope"), (DIFF_WIDTH, "plain"), (LANES, "f32"))
ODD_GROUPS = ((DSA_WIDTH, "rope"), (IDX_HEADS * IDX_DIM, "rope"), (LANES, "rope"), (LANES, "rope"),
              (LANES, "rope"), (DSA_KV_RANK, "kvnorm"),
              (SB_WIDTH, "plain"), (SB_WIDTH, "plain"), (SB_WIDTH, "plain"), (LANES, "f32"))


def _pack_even(w):
    aq, ak, av, af, bq, bk, bv = jnp.split(w, (512, 1024, 1536, 1544, 2056, 2568), axis=1)
    return jnp.concatenate([aq * QK_SCALE, ak, av, bq * QK_SCALE, bk, bv, _pad_cols(af, LANES)],
                           axis=1).astype(BF16)


def _pack_odd(w):
    cq, ckv, ckr, ciq, ciw, cik, sq, sk, sv = jnp.split(w, (512, 640, 656, 912, 916, 980, 1492, 2004), axis=1)
    zeros64 = jnp.zeros_like(cik)
    return jnp.concatenate([cq * QK_SCALE, ciq,
                            jnp.concatenate([cik, zeros64], axis=1), jnp.concatenate([zeros64, cik], axis=1),
                            _pad_cols(ckr, LANES), ckv,
                            sq * QK_SCALE, sk, sv, _pad_cols(ciw * (IDX_HEADS ** -0.5), LANES)],
                           axis=1).astype(BF16)


def _pack_dsa_weights(w_uk, w_uv):
    mq, wuv = [], []
    eye = jnp.eye(ROT_DIM, dtype=F32)
    for h in range(DSA_HEADS):
        r0 = (h % 2) * HEAD_DIM
        m = jnp.zeros((LANES, 2 * LANES), F32)
        m = m.at[r0 + ROT_DIM:r0 + HEAD_DIM, :DSA_KV_RANK].set(w_uk[:, h, :].T)
        m = m.at[r0:r0 + ROT_DIM, DSA_KV_RANK:DSA_KV_RANK + ROT_DIM].set(eye)
        mq.append(m)
        u = jnp.zeros((DSA_KV_RANK, LANES), F32)
        u = u.at[:, r0:r0 + HEAD_DIM].set(w_uv[:, h, :])
        wuv.append(u)
    return jnp.stack(mq).astype(BF16), jnp.stack(wuv).astype(BF16)


def kernel(x, positions, attn_norm, ffn_norm, final_norm, ev_w_in, ev_fgate_b, ev_lambda_q1, ev_lambda_k1,
           ev_lambda_q2, ev_lambda_k2, ev_subln, ev_w_out, od_w_in, od_kv_norm, od_w_uk, od_w_uv, od_w_out,
           ffn_w_in, ffn_conv_w, ffn_conv_b, ffn_w_out):
    batch, seq, _ = x.shape
    depth = attn_norm.shape[0]
    t = batch * seq
    top_k = min(IDX_TOPK_MAX, seq // 4)
    tabs = _rope_tables(positions)
    xf = x.reshape(t, D_MODEL)
    ones_kv = jnp.ones((LANES,), F32)
    for layer in range(depth):
        j = layer // 2
        if layer % 2 == 0:
            aq, ak, av, bq, bk, bv, af = _in_proj(xf, attn_norm[layer], _pack_even(ev_w_in[j]), tabs, ones_kv,
                                                  EVEN_GROUPS)
            c = _fgate_cumsum(af, _pad_cols(ev_fgate_b[j].reshape(1, -1), LANES), batch, seq)
            c8 = c.reshape(batch, seq, LANES)[:, :, :FOX_WIDTH // HEAD_DIM]
            ccol = c8.reshape(batch, seq, -1, 2).transpose(0, 2, 1, 3)
            crow = c8.reshape(batch, seq // ATT_TK, ATT_TK, -1, 2).transpose(0, 3, 1, 4, 2)
            o_a = _fox_attention(aq, ak, av, ccol, crow, batch, seq)
            lam_init = 0.8 - 0.6 * math.exp(-0.3 * layer)
            o_b = _diff_attention(bq, bk, bv, ev_lambda_q1[j], ev_lambda_k1[j], ev_lambda_q2[j], ev_lambda_k2[j],
                                  ev_subln[j], lam_init, batch, seq)
            xf = _out_proj(xf, o_a, o_b, ev_w_out[j].astype(BF16))
        else:
            cq, iq, ika, ikb, kr, ckv, sq, sk, sv, iw = _in_proj(xf, attn_norm[layer], _pack_odd(od_w_in[j]), tabs,
                                                                 od_kv_norm[j], ODD_GROUPS)
            mq, wuv = _pack_dsa_weights(od_w_uk[j], od_w_uv[j])
            o_c = _dsa_attention(iq, iw, cq, ika, ikb, ckv, kr, mq, wuv, top_k, batch, seq)
            o_d = _sb_attention(sq, sk, sv, batch, seq)
            xf = _out_proj(xf, o_c, o_d, od_w_out[j].astype(BF16))
        xf = _ffn(xf, ffn_norm[layer], ffn_w_in[layer].astype(BF16), ffn_conv_w[layer], ffn_conv_b[layer],
                  ffn_w_out[layer].astype(BF16), final_norm, seq, layer == depth - 1)
    return xf.reshape(batch, seq, D_MODEL)
```

```python
import functools
import math

import jax
import jax.numpy as jnp
from jax import lax
from jax.experimental import pallas as pl
from jax.experimental.pallas import tpu as pltpu

F32 = jnp.float32
BF16 = jnp.bfloat16
I32 = jnp.int32

D_MODEL = 1024
HEAD_DIM = 64
ROT_DIM = 16
ROT_HALF = 8
ROPE_THETA = 500000.0
EPS = 1e-6
LANES = 128
QK_SCALE = HEAD_DIM ** -0.5
FOX_WIDTH = 512
DIFF_WIDTH = 512
DIFF_HEADS = 4
DSA_WIDTH = 512
DSA_HEADS = 8
DSA_NOPE = HEAD_DIM - ROT_DIM
DSA_KV_RANK = 128
IDX_HEADS = 4
IDX_DIM = 64
IDX_TOPK_MAX = 256
SB_WIDTH = 512
D_FF = 2816
INT_MIN = -(2 ** 31)

VMEM_LIMIT = 56 * 1024 * 1024

PROJ_TM = 512
PROJ_CHUNK = 256
FFN_TM = 512
FFN_HALO = 16
FFN_CHUNK = 256
ATT_TQ = 256
DSA_TQ = 128


def _cparams(sem):
    return pltpu.CompilerParams(dimension_semantics=sem, vmem_limit_bytes=VMEM_LIMIT)


def _rms_rows(x, g):
    ms = jnp.mean(x * x, axis=-1, keepdims=True)
    return x * lax.rsqrt(ms + EPS) * g


def _rope128(z, c, sa, sb):
    return z * c + pltpu.roll(z, LANES - ROT_HALF, axis=1) * sa + pltpu.roll(z, ROT_HALF, axis=1) * sb


def _in_proj_body(groups, x_ref, g_ref, w_ref, c_ref, sa_ref, sb_ref, kvg_ref, *rest):
    out_refs, h_scr = rest[:-1], rest[-1]
    h_scr[...] = _rms_rows(x_ref[...], g_ref[...]).astype(BF16)
    col = 0
    for (width, kind), o_ref in zip(groups, out_refs):
        cw = min(width, PROJ_CHUNK)
        for c0 in range(0, width, cw):
            zc = jnp.dot(h_scr[...], w_ref[:, col + c0:col + c0 + cw], preferred_element_type=F32)
            for l0 in range(0, cw, LANES):
                z = zc[:, l0:l0 + LANES]
                if kind == "rope":
                    z = _rope128(z, c_ref[...], sa_ref[...], sb_ref[...])
                elif kind == "kvnorm":
                    z = _rms_rows(z, kvg_ref[...])
                o_ref[:, c0 + l0:c0 + l0 + LANES] = z.astype(o_ref.dtype)
        col += width


def _in_proj(x, gain, w_cat, tabs, kv_gain, groups):
    t = x.shape[0]
    n = w_cat.shape[1]
    tm = PROJ_TM
    row = lambda i: (i, 0)
    const = lambda i: (0, 0)
    out_shape = [jax.ShapeDtypeStruct((t, w), F32 if kind == "f32" else BF16) for w, kind in groups]
    out_specs = [pl.BlockSpec((tm, w), row) for w, _ in groups]
    return pl.pallas_call(
        functools.partial(_in_proj_body, groups),
        grid=(t // tm,),
        in_specs=[pl.BlockSpec((tm, D_MODEL), row),
                  pl.BlockSpec((1, D_MODEL), const),
                  pl.BlockSpec((D_MODEL, n), const),
                  pl.BlockSpec((tm, LANES), row),
                  pl.BlockSpec((tm, LANES), row),
                  pl.BlockSpec((tm, LANES), row),
                  pl.BlockSpec((1, LANES), const)],
        out_specs=out_specs,
        out_shape=out_shape,
        scratch_shapes=[pltpu.VMEM((tm, D_MODEL), BF16)],
        compiler_params=_cparams(("parallel",)),
        name="in_proj",
    )(x, gain.reshape(1, D_MODEL), w_cat, tabs[0], tabs[1], tabs[2], kv_gain.reshape(1, LANES))


def _out_proj_body(x_ref, a_ref, b_ref, w_ref, o_ref):
    half = a_ref.shape[1]
    y = jnp.dot(a_ref[...], w_ref[:half, :], preferred_element_type=F32)
    y = y + jnp.dot(b_ref[...], w_ref[half:, :], preferred_element_type=F32)
    o_ref[...] = x_ref[...] + y


def _out_proj(x, a, b, w):
    t = x.shape[0]
    tm = PROJ_TM
    row = lambda i: (i, 0)
    const = lambda i: (0, 0)
    return pl.pallas_call(
        _out_proj_body,
        grid=(t // tm,),
        in_specs=[pl.BlockSpec((tm, D_MODEL), row),
                  pl.BlockSpec((tm, a.shape[1]), row),
                  pl.BlockSpec((tm, b.shape[1]), row),
                  pl.BlockSpec((D_MODEL, D_MODEL), const)],
        out_specs=pl.BlockSpec((tm, D_MODEL), row),
        out_shape=jax.ShapeDtypeStruct((t, D_MODEL), F32),
        compiler_params=_cparams(("parallel",)),
        name="out_proj",
    )(x, a, b, w)


def _ffn_body(tiles_per_seq, final, x_ref, xh_ref, g_ref, win_ref, cw_ref, cb_ref, wout_ref, fg_ref,
              o_ref, h_scr, acc_scr):
    tm = x_ref.shape[0]
    x = x_ref[...]
    g = g_ref[...]
    keep = (pl.program_id(0) % tiles_per_seq != 0).astype(F32)
    h_scr[:FFN_HALO, :] = (_rms_rows(xh_ref[...], g) * keep).astype(BF16)
    h_scr[FFN_HALO:, :] = _rms_rows(x, g).astype(BF16)

    def conv(u, c0):
        y = (pltpu.roll(u, 2, axis=0) * cw_ref[0:1, c0:c0 + FFN_CHUNK]
             + pltpu.roll(u, 1, axis=0) * cw_ref[1:2, c0:c0 + FFN_CHUNK]
             + u * cw_ref[2:3, c0:c0 + FFN_CHUNK])
        return y[FFN_HALO:, :] + cb_ref[0:1, c0:c0 + FFN_CHUNK]

    for ci in range(D_FF // FFN_CHUNK):
        cg = ci * FFN_CHUNK
        cv = D_FF + cg
        ug = jnp.dot(h_scr[...], win_ref[:, cg:cg + FFN_CHUNK], preferred_element_type=F32)
        uv = jnp.dot(h_scr[...], win_ref[:, cv:cv + FFN_CHUNK], preferred_element_type=F32)
        gate = conv(ug, cg)
        val = conv(uv, cv)
        act = (gate * jax.nn.sigmoid(gate) * val).astype(BF16)
        part = jnp.dot(act, wout_ref[cg:cg + FFN_CHUNK, :], preferred_element_type=F32)
        if ci == 0:
            acc_scr[...] = part
        else:
            acc_scr[...] += part
    y = x + acc_scr[...]
    if final:
        y = _rms_rows(y, fg_ref[...])
    o_ref[...] = y


def _ffn(x, gain, w_in, conv_w, conv_b, w_out, final_gain, seq, final):
    t = x.shape[0]
    tm = FFN_TM
    halo_blocks = tm // FFN_HALO
    row = lambda i: (i, 0)
    const = lambda i: (0, 0)
    return pl.pallas_call(
        functools.partial(_ffn_body, seq // tm, final),
        grid=(t // tm,),
        in_specs=[pl.BlockSpec((tm, D_MODEL), row),
                  pl.BlockSpec((FFN_HALO, D_MODEL), lambda i: (jnp.maximum(i * halo_blocks - 1, 0), 0)),
                  pl.BlockSpec((1, D_MODEL), const),
                  pl.BlockSpec((D_MODEL, 2 * D_FF), const),
                  pl.BlockSpec((3, 2 * D_FF), const),
                  pl.BlockSpec((1, 2 * D_FF), const),
                  pl.BlockSpec((D_FF, D_MODEL), const),
                  pl.BlockSpec((1, D_MODEL), const)],
        out_specs=pl.BlockSpec((tm, D_MODEL), row),
        out_shape=jax.ShapeDtypeStruct((t, D_MODEL), F32),
        scratch_shapes=[pltpu.VMEM((tm + FFN_HALO, D_MODEL), BF16), pltpu.VMEM((tm, D_MODEL), F32)],
        compiler_params=_cparams(("parallel",)),
        name="conv_ffn",
    )(x, x, gain.reshape(1, D_MODEL), w_in, conv_w, conv_b.reshape(1, 2 * D_FF), w_out,
      final_gain.reshape(1, D_MODEL))


def _split3(x):
    hi = x.astype(BF16)
    r = x - hi.astype(F32)
    mid = r.astype(BF16)
    lo = (r - mid.astype(F32)).astype(BF16)
    return hi, mid, lo


def _fgate_body(af_ref, b_ref, c_ref):
    s = af_ref.shape[0]
    blk = 512
    ri = lax.broadcasted_iota(I32, (blk, blk), 0)
    ci = lax.broadcasted_iota(I32, (blk, blk), 1)
    tri = jnp.where(ci <= ri, 1.0, 0.0).astype(BF16)
    carry = jnp.zeros((1, LANES), F32)
    for r0 in range(0, s, blk):
        lf = jax.nn.log_sigmoid(af_ref[r0:r0 + blk, :] + b_ref[...])
        hi, mid, lo = _split3(lf)
        c = (jnp.dot(tri, hi, preferred_element_type=F32)
             + jnp.dot(tri, mid, preferred_element_type=F32)
             + jnp.dot(tri, lo, preferred_element_type=F32)) + carry
        c_ref[r0:r0 + blk, :] = c
        carry = c[blk - 1:blk, :]


def _fgate_cumsum(af, bias, batch, seq):
    return pl.pallas_call(
        _fgate_body,
        grid=(batch,),
        in_specs=[pl.BlockSpec((seq, LANES), lambda b: (b, 0)),
                  pl.BlockSpec((1, LANES), lambda b: (0, 0))],
        out_specs=pl.BlockSpec((seq, LANES), lambda b: (b, 0)),
        out_shape=jax.ShapeDtypeStruct((batch * seq, LANES), F32),
        compiler_params=_cparams(("parallel",)),
        name="fox_gate_cumsum",
    )(af, bias)


def _nt_dot(a, b):
    return lax.dot_general(a, b, (((1,), (1,)), ((), ())), preferred_element_type=F32)


def _per_query_block(qi, n, fn):
    for c in range(n):
        @pl.when(qi == c)
        def _():
            fn(c)


def _stack_heads(q):
    lane = lax.broadcasted_iota(I32, (1, LANES), 1)
    zero = jnp.zeros_like(q)
    return jnp.concatenate([jnp.where(lane < HEAD_DIM, q, zero), jnp.where(lane >= HEAD_DIM, q, zero)], axis=0)


def _unstack_heads(o):
    tq = o.shape[0] // 2
    lane = lax.broadcasted_iota(I32, (1, LANES), 1)
    return jnp.where(lane < HEAD_DIM, o[:tq], o[tq:])


def _diag_mask(rows, tq, strict):
    qpos = lax.broadcasted_iota(I32, (rows, tq), 0) & (tq - 1)
    kpos = lax.broadcasted_iota(I32, (rows, tq), 1)
    return kpos < qpos if strict else kpos <= qpos


def _fill_diag(s, tq, strict, fill):
    nk = s.shape[1]
    d = jnp.where(_diag_mask(s.shape[0], tq, strict), s[:, nk - tq:], fill)
    return d if nk == tq else jnp.concatenate([s[:, :nk - tq], d], axis=1)


def _softmax_pv(s, v):
    m = jnp.max(s, axis=-1, keepdims=True)
    e = jnp.exp(s - m)
    l = jnp.sum(e, axis=-1, keepdims=True)
    return jnp.dot(e.astype(BF16), v, preferred_element_type=F32) / l


def _fox_block(c, q_ref, k_ref, v_ref, ccol_ref, crow_ref, o_ref):
    tq = ATT_TQ
    nk = (c + 1) * tq
    s = _nt_dot(_stack_heads(q_ref[...]), k_ref[0:nk, :])
    bias = jnp.concatenate([ccol_ref[:, 0:1] - crow_ref[0:1, 0:nk],
                            ccol_ref[:, 1:2] - crow_ref[1:2, 0:nk]], axis=0)
    s = _fill_diag(s + bias, tq, False, -jnp.inf)
    o_ref[...] = _unstack_heads(_softmax_pv(s, v_ref[0:nk, :])).astype(o_ref.dtype)


def _fox_body(*refs):
    _per_query_block(pl.program_id(2), refs[1].shape[0] // ATT_TQ, lambda c: _fox_block(c, *refs))


def _fox_attention(q, k, v, ccol, crow, batch, seq):
    tq = ATT_TQ
    nq = seq // tq
    pairs = q.shape[1] // LANES
    qmap = lambda b, p, i: (b * nq + i, p)
    kvmap = lambda b, p, i: (b, p)
    return pl.pallas_call(
        _fox_body,
        grid=(batch, pairs, nq),
        in_specs=[pl.BlockSpec((tq, LANES), qmap),
                  pl.BlockSpec((seq, LANES), kvmap),
                  pl.BlockSpec((seq, LANES), kvmap),
                  pl.BlockSpec((None, None, tq, 2), lambda b, p, i: (b, p, i, 0)),
                  pl.BlockSpec((None, None, 2, seq), lambda b, p, i: (b, p, 0, 0))],
        out_specs=pl.BlockSpec((tq, LANES), qmap),
        out_shape=jax.ShapeDtypeStruct(q.shape, BF16),
        compiler_params=_cparams(("parallel", "parallel", "arbitrary")),
        name="fox_attention",
    )(q, k, v, ccol, crow)


def _diff_block(c, lam_init, q_ref, k_ref, v_ref, lq1_ref, lk1_ref, lq2_ref, lk2_ref, sub_ref, o_ref):
    tq = ATT_TQ
    nk = (c + 1) * tq
    s = _nt_dot(_stack_heads(q_ref[...]), k_ref[0:nk, :])
    o = _softmax_pv(_fill_diag(s, tq, False, -jnp.inf), v_ref[0:nk, :])
    lam = (jnp.exp(jnp.sum(lq1_ref[...] * lk1_ref[...], axis=-1, keepdims=True))
           - jnp.exp(jnp.sum(lq2_ref[...] * lk2_ref[...], axis=-1, keepdims=True)) + lam_init)
    o = o[:tq] - lam * o[tq:]
    o_ref[...] = (_rms_rows(o, sub_ref[...]) * (1.0 - lam_init)).astype(o_ref.dtype)


def _diff_body(lam_init, *refs):
    _per_query_block(pl.program_id(2), refs[1].shape[0] // ATT_TQ, lambda c: _diff_block(c, lam_init, *refs))


def _diff_attention(q, k, v, lq1, lk1, lq2, lk2, subln, lam_init, batch, seq):
    tq = ATT_TQ
    nq = seq // tq
    heads = q.shape[1] // LANES
    qmap = lambda b, h, i: (b * nq + i, h)
    kvmap = lambda b, h, i: (b, h)
    vec = lambda n: pl.BlockSpec((1, n), lambda b, h, i: (0, 0))
    return pl.pallas_call(
        functools.partial(_diff_body, lam_init),
        grid=(batch, heads, nq),
        in_specs=[pl.BlockSpec((tq, LANES), qmap),
                  pl.BlockSpec((seq, LANES), kvmap),
                  pl.BlockSpec((seq, LANES), kvmap),
                  vec(HEAD_DIM), vec(HEAD_DIM), vec(HEAD_DIM), vec(HEAD_DIM), vec(LANES)],
        out_specs=pl.BlockSpec((tq, LANES), qmap),
        out_shape=jax.ShapeDtypeStruct(q.shape, BF16),
        compiler_params=_cparams(("parallel", "parallel", "arbitrary")),
        name="diff_attention",
    )(q, k, v, lq1.reshape(1, -1), lk1.reshape(1, -1), lq2.reshape(1, -1), lk2.reshape(1, -1),
      subln.reshape(1, -1))


def _sb_block(c, q_ref, k_ref, v_ref, o_ref):
    tq = tk = ATT_TQ
    nk = (c + 1) * tq
    z = _nt_dot(_stack_heads(q_ref[...]), k_ref[0:nk, :])
    log1m = -(jnp.maximum(z, 0.0) + jnp.log1p(jnp.exp(-jnp.abs(z))))
    log1m = _fill_diag(log1m, tq, True, 0.0)
    ri = lax.broadcasted_iota(I32, (2 * tk, tk), 0) & (tk - 1)
    ci = lax.broadcasted_iota(I32, (2 * tk, tk), 1)
    later = jnp.where(ri > ci, 1.0, 0.0).astype(BF16)
    run = jnp.zeros((2 * tq, 1), F32)
    acc = jnp.zeros((2 * tq, LANES), F32)
    for blk in reversed(range(nk // tk)):
        lb = log1m[:, blk * tk:(blk + 1) * tk]
        hi = lb.astype(BF16)
        lo = (lb - hi.astype(F32)).astype(BF16)
        local = jnp.dot(jnp.concatenate([hi, lo], axis=1), later, preferred_element_type=F32)
        a = jnp.exp(z[:, blk * tk:(blk + 1) * tk] + lb + (local + run))
        if blk == c:
            a = jnp.where(_diag_mask(2 * tq, tq, True), a, 0.0)
        acc = acc + jnp.dot(a.astype(BF16), v_ref[blk * tk:(blk + 1) * tk, :], preferred_element_type=F32)
        run = run + (local[:, 0:1] + lb[:, 0:1])
    o_ref[...] = _unstack_heads(acc).astype(o_ref.dtype)


def _sb_body(*refs):
    _per_query_block(pl.program_id(2), refs[1].shape[0] // ATT_TQ, lambda c: _sb_block(c, *refs))


def _sb_attention(q, k, v, batch, seq):
    tq = ATT_TQ
    nq = seq // tq
    pairs = q.shape[1] // LANES
    qmap = lambda b, p, i: (b * nq + i, p)
    kvmap = lambda b, p, i: (b, p)
    return pl.pallas_call(
        _sb_body,
        grid=(batch, pairs, nq),
        in_specs=[pl.BlockSpec((tq, LANES), qmap),
                  pl.BlockSpec((seq, LANES), kvmap),
                  pl.BlockSpec((seq, LANES), kvmap)],
        out_specs=pl.BlockSpec((tq, LANES), qmap),
        out_shape=jax.ShapeDtypeStruct(q.shape, BF16),
        compiler_params=_cparams(("parallel", "parallel", "arbitrary")),
        name="stickbreak_attention",
    )(q, k, v)


def _count(pred):
    return jnp.sum(jnp.where(pred, 1.0, 0.0), axis=-1, keepdims=True)


def _dsa_block(nk, top_k, q0, iq_ref, iw_ref, cq_ref, ika_ref, ikb_ref, ckv_ref, kr_ref, mq_ref, wuv_ref,
               o_ref, cut_scr):
    tq = DSA_TQ
    edge = 2 * tq
    causal = (lax.broadcasted_iota(I32, (tq, edge), 1) + (nk - edge)
              <= q0 + lax.broadcasted_iota(I32, (tq, edge), 0))

    def on_edge(x, fn):
        return jnp.concatenate([x[:, :nk - edge], fn(x[:, nk - edge:])], axis=1) if nk > edge else fn(x)

    iq = iq_ref[...]
    iq_st = jnp.concatenate([iq[:, :LANES], iq[:, LANES:]], axis=0)
    la = jnp.maximum(_nt_dot(iq_st, ika_ref[0:nk, :]), 0.0)
    lb = jnp.maximum(_nt_dot(iq_st, ikb_ref[0:nk, :]), 0.0)
    score = (la[:tq] * iw_ref[:, 0:1] + lb[:tq] * iw_ref[:, 1:2]
             + la[tq:] * iw_ref[:, 2:3] + lb[tq:] * iw_ref[:, 3:4])

    bits = pltpu.bitcast(score, I32)
    keys = bits ^ ((bits >> 31) & jnp.int32(0x7FFFFFFF))
    keys = jnp.where(bits == jnp.int32(INT_MIN), 0, keys)
    keys = on_edge(keys, lambda x: jnp.where(causal, x, jnp.int32(INT_MIN)))

    tu = jnp.zeros((tq, 1), I32)
    for bit in reversed(range(32)):
        cand = tu | jnp.int32(INT_MIN if bit == 31 else 1 << bit)
        tu = jnp.where(_count(keys >= (cand ^ jnp.int32(INT_MIN))) >= top_k, cand, tu)
    thr = tu ^ jnp.int32(INT_MIN)
    above = keys > thr
    tied = keys == thr
    need = top_k - _count(above)

    excess = jnp.where(thr == jnp.int32(INT_MIN), 0.0, _count(tied) - need)
    cut_scr[...] = jnp.full(cut_scr.shape, nk, I32)

    @pl.when(jnp.max(excess) > 0.0)
    def _():
        kpos = lax.broadcasted_iota(I32, (tq, nk), 1)
        index_bits = (nk - 1).bit_length()

        def index_bit(i, x):
            cand = x | lax.shift_left(jnp.int32(1), index_bits - 1 - i)
            return jnp.where(_count(tied & (kpos < cand)) < need, cand, x)

        cut = lax.fori_loop(0, index_bits, index_bit, jnp.zeros((tq, 1), I32))
        cut_scr[...] = jnp.broadcast_to(cut, cut_scr.shape)

    kpos = lax.broadcasted_iota(I32, (tq, nk), 1)
    sel = jnp.where(above | (tied & (kpos <= cut_scr[:, 0:1])), 0.0, -jnp.inf)
    sel = on_edge(sel, lambda x: jnp.where(causal, x, -jnp.inf))

    kv = jnp.concatenate([ckv_ref[0:nk, :], kr_ref[0:nk, :]], axis=1)
    half = DSA_HEADS // 4
    for g in range(2):
        qc = []
        for p in range(g * half, (g + 1) * half):
            t = jnp.dot(cq_ref[:, p * LANES:(p + 1) * LANES], mq_ref[p], preferred_element_type=F32)
            qc += [t[:, :2 * LANES], t[:, 2 * LANES:]]
        qcat = jnp.concatenate(qc, axis=0).astype(BF16)
        s = _nt_dot(qcat, kv)
        s = jnp.concatenate([s[i * tq:(i + 1) * tq] + sel for i in range(2 * half)], axis=0)
        o_lat = _softmax_pv(s, ckv_ref[0:nk, :]).astype(BF16)
        for pi in range(half):
            p = g * half + pi
            pair = jnp.concatenate([o_lat[(2 * pi) * tq:(2 * pi + 1) * tq],
                                    o_lat[(2 * pi + 1) * tq:(2 * pi + 2) * tq]], axis=1)
            o_ref[:, p * LANES:(p + 1) * LANES] = jnp.dot(
                pair, wuv_ref[p], preferred_element_type=F32).astype(o_ref.dtype)


def _dsa_body(seq, top_k, *refs):
    qi = pl.program_id(1)
    blocks_per_branch = 2
    for c in range(seq // (blocks_per_branch * DSA_TQ)):
        @pl.when(qi // blocks_per_branch == c)
        def _():
            _dsa_block((c + 1) * blocks_per_branch * DSA_TQ, top_k, qi * DSA_TQ, *refs)


def _dsa_attention(iq, iw, cq, ika, ikb, ckv, kr, mq, wuv, top_k, batch, seq):
    tq = DSA_TQ
    nq = seq // tq
    qmap = lambda b, i: (b * nq + i, 0)
    kmap = lambda b, i: (b, 0)
    cmap = lambda b, i: (0, 0, 0)
    return pl.pallas_call(
        functools.partial(_dsa_body, seq, top_k),
        grid=(batch, nq),
        in_specs=[pl.BlockSpec((tq, iq.shape[1]), qmap),
                  pl.BlockSpec((tq, LANES), qmap),
                  pl.BlockSpec((tq, DSA_WIDTH), qmap),
                  pl.BlockSpec((seq, LANES), kmap),
                  pl.BlockSpec((seq, LANES), kmap),
                  pl.BlockSpec((seq, LANES), kmap),
                  pl.BlockSpec((seq, LANES), kmap),
                  pl.BlockSpec(mq.shape, cmap),
                  pl.BlockSpec(wuv.shape, cmap)],
        out_specs=pl.BlockSpec((tq, DSA_WIDTH), qmap),
        out_shape=jax.ShapeDtypeStruct(cq.shape, BF16),
        scratch_shapes=[pltpu.VMEM((tq, LANES), I32)],
        compiler_params=_cparams(("parallel", "arbitrary")),
        name="dsa_attention",
    )(iq, iw, cq, ika, ikb, ckv, kr, mq, wuv)


def _rope_tables(positions):
    inv_freq = ROPE_THETA ** (-jnp.arange(0, ROT_DIM, 2, dtype=F32) / ROT_DIM)
    ang = positions.astype(F32).reshape(-1, 1) * inv_freq
    cos, sin = jnp.cos(ang), jnp.sin(ang)
    t = ang.shape[0]
    pad = jnp.zeros((t, HEAD_DIM - ROT_DIM), F32)
    zero8 = jnp.zeros((t, ROT_HALF), F32)
    c64 = jnp.concatenate([cos, cos, pad + 1.0], axis=1)
    sa64 = jnp.concatenate([-sin, zero8, pad], axis=1)
    sb64 = jnp.concatenate([zero8, sin, pad], axis=1)
    return tuple(jnp.tile(a, (1, LANES // HEAD_DIM)) for a in (c64, sa64, sb64))


def _pad_cols(w, width):
    return jnp.pad(w, ((0, 0), (0, width - w.shape[1])))


EVEN_GROUPS = ((FOX_WIDTH, "plain"), (FOX_WIDTH, "plain"), (FOX_WIDTH, "plain"),
               (DIFF_WIDTH, "rope"), (DIFF_WIDTH, "rope"), (DIFF_WIDTH, "plain"), (LANES, "f32"))
ODD_GROUPS = ((DSA_WIDTH, "rope"), (IDX_HEADS * IDX_DIM, "rope"), (LANES, "rope"), (LANES, "rope"),
              (LANES, "rope"), (DSA_KV_RANK, "kvnorm"),
              (SB_WIDTH, "plain"), (SB_WIDTH, "plain"), (SB_WIDTH, "plain"), (LANES, "f32"))


def _pack_even(w):
    aq, ak, av, af, bq, bk, bv = jnp.split(w, (512, 1024, 1536, 1544, 2056, 2568), axis=1)
    return jnp.concatenate([aq * QK_SCALE, ak, av, bq * QK_SCALE, bk, bv, _pad_cols(af, LANES)],
                           axis=1).astype(BF16)


def _pack_odd(w):
    cq, ckv, ckr, ciq, ciw, cik, sq, sk, sv = jnp.split(w, (512, 640, 656, 912, 916, 980, 1492, 2004), axis=1)
    zeros64 = jnp.zeros_like(cik)
    return jnp.concatenate([cq * QK_SCALE, ciq,
                            jnp.concatenate([cik, zeros64], axis=1), jnp.concatenate([zeros64, cik], axis=1),
                            _pad_cols(ckr, LANES), ckv,
                            sq * QK_SCALE, sk, sv, _pad_cols(ciw * (IDX_HEADS ** -0.5), LANES)],
                           axis=1).astype(BF16)


def _pack_dsa_weights(w_uk, w_uv):
    mq, wuv = [], []
    eye = jnp.eye(ROT_DIM, dtype=F32)
    for p in range(DSA_HEADS // 2):
        m = jnp.zeros((LANES, 4 * LANES), F32)
        u = jnp.zeros((2 * DSA_KV_RANK, LANES), F32)
        for hh in range(2):
            h = 2 * p + hh
            r0 = hh * HEAD_DIM
            c0 = hh * 2 * LANES
            m = m.at[r0 + ROT_DIM:r0 + HEAD_DIM, c0:c0 + DSA_KV_RANK].set(w_uk[:, h, :].T)
            m = m.at[r0:r0 + ROT_DIM, c0 + DSA_KV_RANK:c0 + DSA_KV_RANK + ROT_DIM].set(eye)
            u = u.at[hh * DSA_KV_RANK:(hh + 1) * DSA_KV_RANK, r0:r0 + HEAD_DIM].set(w_uv[:, h, :])
        mq.append(m)
        wuv.append(u)
    return jnp.stack(mq).astype(BF16), jnp.stack(wuv).astype(BF16)


def kernel(x, positions, attn_norm, ffn_norm, final_norm, ev_w_in, ev_fgate_b, ev_lambda_q1, ev_lambda_k1,
           ev_lambda_q2, ev_lambda_k2, ev_subln, ev_w_out, od_w_in, od_kv_norm, od_w_uk, od_w_uv, od_w_out,
           ffn_w_in, ffn_conv_w, ffn_conv_b, ffn_w_out):
    batch, seq, _ = x.shape
    depth = attn_norm.shape[0]
    t = batch * seq
    top_k = min(IDX_TOPK_MAX, seq // 4)
    tabs = _rope_tables(positions)
    xf = x.reshape(t, D_MODEL)
    ones_kv = jnp.ones((LANES,), F32)
    for layer in range(depth):
        j = layer // 2
        if layer % 2 == 0:
            aq, ak, av, bq, bk, bv, af = _in_proj(xf, attn_norm[layer], _pack_even(ev_w_in[j]), tabs, ones_kv,
                                                  EVEN_GROUPS)
            c = _fgate_cumsum(af, _pad_cols(ev_fgate_b[j].reshape(1, -1), LANES), batch, seq)
            c8 = c.reshape(batch, seq, LANES)[:, :, :FOX_WIDTH // HEAD_DIM]
            ccol = c8.reshape(batch, seq, -1, 2).transpose(0, 2, 1, 3)
            crow = c8.reshape(batch, seq, -1, 2).transpose(0, 2, 3, 1)
            o_a = _fox_attention(aq, ak, av, ccol, crow, batch, seq)
            lam_init = 0.8 - 0.6 * math.exp(-0.3 * layer)
            o_b = _diff_attention(bq, bk, bv, ev_lambda_q1[j], ev_lambda_k1[j], ev_lambda_q2[j], ev_lambda_k2[j],
                                  ev_subln[j], lam_init, batch, seq)
            xf = _out_proj(xf, o_a, o_b, ev_w_out[j].astype(BF16))
        else:
            cq, iq, ika, ikb, kr, ckv, sq, sk, sv, iw = _in_proj(xf, attn_norm[layer], _pack_odd(od_w_in[j]), tabs,
                                                                 od_kv_norm[j], ODD_GROUPS)
            mq, wuv = _pack_dsa_weights(od_w_uk[j], od_w_uv[j])
            o_c = _dsa_attention(iq, iw, cq, ika, ikb, ckv, kr, mq, wuv, top_k, batch, seq)
            o_d = _sb_attention(sq, sk, sv, batch, seq)
            xf = _out_proj(xf, o_c, o_d, od_w_out[j].astype(BF16))
        xf = _ffn(xf, ffn_norm[layer], ffn_w_in[layer].astype(BF16), ffn_conv_w[layer], ffn_conv_b[layer],
                  ffn_w_out[layer].astype(BF16), final_norm, seq, layer == depth - 1)
    return xf.reshape(batch, seq, D_MODEL)
```

```python
import functools
import math

import jax
import jax.numpy as jnp
from jax import lax
from jax.experimental import pallas as pl
from jax.experimental.pallas import tpu as pltpu

F32 = jnp.float32
BF16 = jnp.bfloat16
I32 = jnp.int32

D_MODEL = 1024
HEAD_DIM = 64
ROT_DIM = 16
ROT_HALF = 8
ROPE_THETA = 500000.0
EPS = 1e-6
LANES = 128
QK_SCALE = HEAD_DIM ** -0.5
FOX_WIDTH = 512
DIFF_WIDTH = 512
DIFF_HEADS = 4
DSA_WIDTH = 512
DSA_HEADS = 8
DSA_NOPE = HEAD_DIM - ROT_DIM
DSA_KV_RANK = 128
IDX_HEADS = 4
IDX_DIM = 64
IDX_TOPK_MAX = 256
SB_WIDTH = 512
D_FF = 2816
INT_MIN = -(2 ** 31)

VMEM_LIMIT = 56 * 1024 * 1024

PROJ_TM = 512
PROJ_CHUNK = 256
FFN_TM = 512
FFN_HALO = 16
FFN_CHUNK = 256
ATT_TQ = 256
ATT_CHUNK = 512
DSA_TQ = 256
DSA_STEPS_PER_BRANCH = 2


def _cparams(sem):
    return pltpu.CompilerParams(dimension_semantics=sem, vmem_limit_bytes=VMEM_LIMIT)


def _rms_rows(x, g):
    ms = jnp.mean(x * x, axis=-1, keepdims=True)
    return x * lax.rsqrt(ms + EPS) * g


def _rope128(z, c, sa, sb):
    return z * c + pltpu.roll(z, LANES - ROT_HALF, axis=1) * sa + pltpu.roll(z, ROT_HALF, axis=1) * sb


def _in_proj_body(groups, x_ref, g_ref, w_ref, c_ref, sa_ref, sb_ref, kvg_ref, *rest):
    out_refs, h_scr = rest[:-1], rest[-1]
    h_scr[...] = _rms_rows(x_ref[...], g_ref[...]).astype(BF16)
    col = 0
    for (width, kind, block_major), o_ref in zip(groups, out_refs):
        cw = min(width, PROJ_CHUNK)
        for c0 in range(0, width, cw):
            zc = jnp.dot(h_scr[...], w_ref[:, col + c0:col + c0 + cw], preferred_element_type=F32)
            for l0 in range(0, cw, LANES):
                z = zc[:, l0:l0 + LANES]
                if kind == "rope" or (kind == "kv" and c0 + l0 == DSA_KV_RANK):
                    z = _rope128(z, c_ref[...], sa_ref[...], sb_ref[...])
                elif kind == "kv":
                    z = _rms_rows(z, kvg_ref[...])
                if block_major:
                    o_ref[(c0 + l0) // LANES] = z.astype(o_ref.dtype)
                else:
                    o_ref[:, c0 + l0:c0 + l0 + LANES] = z.astype(o_ref.dtype)
        col += width


def _in_proj(x, gain, w_cat, tabs, kv_gain, groups):
    t = x.shape[0]
    n = w_cat.shape[1]
    tm = PROJ_TM
    row = lambda i: (i, 0)
    const = lambda i: (0, 0)
    out_shape, out_specs = [], []
    for w, kind, block_major in groups:
        dtype = F32 if kind == "f32" else BF16
        if block_major:
            out_shape.append(jax.ShapeDtypeStruct((w // LANES, t, LANES), dtype))
            out_specs.append(pl.BlockSpec((w // LANES, tm, LANES), lambda i: (0, i, 0)))
        else:
            out_shape.append(jax.ShapeDtypeStruct((t, w), dtype))
            out_specs.append(pl.BlockSpec((tm, w), row))
    return pl.pallas_call(
        functools.partial(_in_proj_body, groups),
        grid=(t // tm,),
        in_specs=[pl.BlockSpec((tm, D_MODEL), row),
                  pl.BlockSpec((1, D_MODEL), const),
                  pl.BlockSpec((D_MODEL, n), const),
                  pl.BlockSpec((tm, LANES), row),
                  pl.BlockSpec((tm, LANES), row),
                  pl.BlockSpec((tm, LANES), row),
                  pl.BlockSpec((1, LANES), const)],
        out_specs=out_specs,
        out_shape=out_shape,
        scratch_shapes=[pltpu.VMEM((tm, D_MODEL), BF16)],
        compiler_params=_cparams(("parallel",)),
        name="in_proj",
    )(x, gain.reshape(1, D_MODEL), w_cat, tabs[0], tabs[1], tabs[2], kv_gain.reshape(1, LANES))


def _out_proj_body(x_ref, a_ref, b_ref, w_ref, o_ref):
    if len(a_ref.shape) == 3:
        a = jnp.concatenate([a_ref[i] for i in range(a_ref.shape[0])], axis=1)
    else:
        a = a_ref[...]
    half = a.shape[1]
    y = jnp.dot(a, w_ref[:half, :], preferred_element_type=F32)
    y = y + jnp.dot(b_ref[...], w_ref[half:, :], preferred_element_type=F32)
    o_ref[...] = x_ref[...] + y


def _out_proj(x, a, b, w):
    t = x.shape[0]
    tm = PROJ_TM
    row = lambda i: (i, 0)
    const = lambda i: (0, 0)
    a_spec = (pl.BlockSpec((a.shape[0], tm, LANES), lambda i: (0, i, 0)) if a.ndim == 3
              else pl.BlockSpec((tm, a.shape[1]), row))
    return pl.pallas_call(
        _out_proj_body,
        grid=(t // tm,),
        in_specs=[pl.BlockSpec((tm, D_MODEL), row),
                  a_spec,
                  pl.BlockSpec((tm, b.shape[1]), row),
                  pl.BlockSpec((D_MODEL, D_MODEL), const)],
        out_specs=pl.BlockSpec((tm, D_MODEL), row),
        out_shape=jax.ShapeDtypeStruct((t, D_MODEL), F32),
        compiler_params=_cparams(("parallel",)),
        name="out_proj",
    )(x, a, b, w)


def _ffn_body(tiles_per_seq, final, x_ref, xh_ref, g_ref, win_ref, cw_ref, cb_ref, wout_ref, fg_ref,
              o_ref, h_scr, acc_scr):
    tm = x_ref.shape[0]
    x = x_ref[...]
    g = g_ref[...]
    keep = (pl.program_id(0) % tiles_per_seq != 0).astype(F32)
    h_scr[:FFN_HALO, :] = (_rms_rows(xh_ref[...], g) * keep).astype(BF16)
    h_scr[FFN_HALO:, :] = _rms_rows(x, g).astype(BF16)

    def conv(u, c0):
        y = (pltpu.roll(u, 2, axis=0) * cw_ref[0:1, c0:c0 + FFN_CHUNK]
             + pltpu.roll(u, 1, axis=0) * cw_ref[1:2, c0:c0 + FFN_CHUNK]
             + u * cw_ref[2:3, c0:c0 + FFN_CHUNK])
        return y[FFN_HALO:, :] + cb_ref[0:1, c0:c0 + FFN_CHUNK]

    for ci in range(D_FF // FFN_CHUNK):
        cg = ci * FFN_CHUNK
        cv = D_FF + cg
        ug = jnp.dot(h_scr[...], win_ref[:, cg:cg + FFN_CHUNK], preferred_element_type=F32)
        uv = jnp.dot(h_scr[...], win_ref[:, cv:cv + FFN_CHUNK], preferred_element_type=F32)
        gate = conv(ug, cg)
        val = conv(uv, cv)
        act = (gate * jax.nn.sigmoid(gate) * val).astype(BF16)
        part = jnp.dot(act, wout_ref[cg:cg + FFN_CHUNK, :], preferred_element_type=F32)
        if ci == 0:
            acc_scr[...] = part
        else:
            acc_scr[...] += part
    y = x + acc_scr[...]
    if final:
        y = _rms_rows(y, fg_ref[...])
    o_ref[...] = y


def _ffn(x, gain, w_in, conv_w, conv_b, w_out, final_gain, seq, final):
    t = x.shape[0]
    tm = FFN_TM
    halo_blocks = tm // FFN_HALO
    row = lambda i: (i, 0)
    const = lambda i: (0, 0)
    return pl.pallas_call(
        functools.partial(_ffn_body, seq // tm, final),
        grid=(t // tm,),
        in_specs=[pl.BlockSpec((tm, D_MODEL), row),
                  pl.BlockSpec((FFN_HALO, D_MODEL), lambda i: (jnp.maximum(i * halo_blocks - 1, 0), 0)),
                  pl.BlockSpec((1, D_MODEL), const),
                  pl.BlockSpec((D_MODEL, 2 * D_FF), const),
                  pl.BlockSpec((3, 2 * D_FF), const),
                  pl.BlockSpec((1, 2 * D_FF), const),
                  pl.BlockSpec((D_FF, D_MODEL), const),
                  pl.BlockSpec((1, D_MODEL), const)],
        out_specs=pl.BlockSpec((tm, D_MODEL), row),
        out_shape=jax.ShapeDtypeStruct((t, D_MODEL), F32),
        scratch_shapes=[pltpu.VMEM((tm + FFN_HALO, D_MODEL), BF16), pltpu.VMEM((tm, D_MODEL), F32)],
        compiler_params=_cparams(("parallel",)),
        name="conv_ffn",
    )(x, x, gain.reshape(1, D_MODEL), w_in, conv_w, conv_b.reshape(1, 2 * D_FF), w_out,
      final_gain.reshape(1, D_MODEL))


def _split3(x):
    hi = x.astype(BF16)
    r = x - hi.astype(F32)
    mid = r.astype(BF16)
    lo = (r - mid.astype(F32)).astype(BF16)
    return hi, mid, lo


def _fgate_body(af_ref, b_ref, c_ref):
    s = af_ref.shape[0]
    blk = 512
    ri = lax.broadcasted_iota(I32, (blk, blk), 0)
    ci = lax.broadcasted_iota(I32, (blk, blk), 1)
    tri = jnp.where(ci <= ri, 1.0, 0.0).astype(BF16)
    carry = jnp.zeros((1, LANES), F32)
    for r0 in range(0, s, blk):
        lf = jax.nn.log_sigmoid(af_ref[r0:r0 + blk, :] + b_ref[...])
        hi, mid, lo = _split3(lf)
        c = (jnp.dot(tri, hi, preferred_element_type=F32)
             + jnp.dot(tri, mid, preferred_element_type=F32)
             + jnp.dot(tri, lo, preferred_element_type=F32)) + carry
        c_ref[r0:r0 + blk, :] = c
        carry = c[blk - 1:blk, :]


def _fgate_cumsum(af, bias, batch, seq):
    return pl.pallas_call(
        _fgate_body,
        grid=(batch,),
        in_specs=[pl.BlockSpec((seq, LANES), lambda b: (b, 0)),
                  pl.BlockSpec((1, LANES), lambda b: (0, 0))],
        out_specs=pl.BlockSpec((seq, LANES), lambda b: (b, 0)),
        out_shape=jax.ShapeDtypeStruct((batch * seq, LANES), F32),
        compiler_params=_cparams(("parallel",)),
        name="fox_gate_cumsum",
    )(af, bias)


def _nt_dot(a, b):
    return lax.dot_general(a, b, (((1,), (1,)), ((), ())), preferred_element_type=F32)


def _per_query_block(qi, n, fn):
    for c in range(n):
        @pl.when(qi == c)
        def _():
            fn(c)


def _stack_heads(q):
    lane = lax.broadcasted_iota(I32, (1, LANES), 1)
    zero = jnp.zeros_like(q)
    return jnp.concatenate([jnp.where(lane < HEAD_DIM, q, zero), jnp.where(lane >= HEAD_DIM, q, zero)], axis=0)


def _unstack_heads(o):
    tq = o.shape[0] // 2
    lane = lax.broadcasted_iota(I32, (1, LANES), 1)
    return jnp.where(lane < HEAD_DIM, o[:tq], o[tq:])


def _diag_mask(rows, tq, strict):
    qpos = lax.broadcasted_iota(I32, (rows, tq), 0) & (tq - 1)
    kpos = lax.broadcasted_iota(I32, (rows, tq), 1)
    return kpos < qpos if strict else kpos <= qpos


def _fill_diag(s, tq, strict, fill):
    nk = s.shape[1]
    d = jnp.where(_diag_mask(s.shape[0], tq, strict), s[:, nk - tq:], fill)
    return d if nk == tq else jnp.concatenate([s[:, :nk - tq], d], axis=1)


def _softmax_pv(s, v):
    m = jnp.max(s, axis=-1, keepdims=True)
    e = jnp.exp(s - m)
    l = jnp.sum(e, axis=-1, keepdims=True)
    return jnp.dot(e.astype(BF16), v, preferred_element_type=F32) / l


def _online_softmax_pv(rows, nk, score_chunk, v_ref):
    m = jnp.full((rows, 1), -jnp.inf, F32)
    l = jnp.zeros((rows, 1), F32)
    acc = jnp.zeros((rows, LANES), F32)
    for k0 in range(0, nk, ATT_CHUNK):
        k1 = min(k0 + ATT_CHUNK, nk)
        s = score_chunk(k0, k1)
        m_new = jnp.maximum(m, jnp.max(s, axis=-1, keepdims=True))
        p = jnp.exp(s - m_new)
        alpha = jnp.exp(m - m_new)
        l = alpha * l + jnp.sum(p, axis=-1, keepdims=True)
        acc = alpha * acc + jnp.dot(p.astype(BF16), v_ref[k0:k1, :], preferred_element_type=F32)
        m = m_new
    return acc / l


def _fox_block(c, q_ref, k_ref, v_ref, ccol_ref, crow_ref, o_ref):
    tq = ATT_TQ
    nk = (c + 1) * tq
    q_st = _stack_heads(q_ref[...])

    def score_chunk(k0, k1):
        s = _nt_dot(q_st, k_ref[k0:k1, :])
        bias = jnp.concatenate([ccol_ref[:, 0:1] - crow_ref[0:1, k0:k1],
                                ccol_ref[:, 1:2] - crow_ref[1:2, k0:k1]], axis=0)
        return _fill_diag(s + bias, tq, False, -jnp.inf) if k1 == nk else s + bias

    o_ref[...] = _unstack_heads(_online_softmax_pv(2 * tq, nk, score_chunk, v_ref)).astype(o_ref.dtype)


def _fox_body(*refs):
    _per_query_block(pl.program_id(2), refs[1].shape[0] // ATT_TQ, lambda c: _fox_block(c, *refs))


def _fox_attention(q, k, v, ccol, crow, batch, seq):
    tq = ATT_TQ
    nq = seq // tq
    pairs = q.shape[1] // LANES
    qmap = lambda b, p, i: (b * nq + i, p)
    kvmap = lambda b, p, i: (b, p)
    return pl.pallas_call(
        _fox_body,
        grid=(batch, pairs, nq),
        in_specs=[pl.BlockSpec((tq, LANES), qmap),
                  pl.BlockSpec((seq, LANES), kvmap),
                  pl.BlockSpec((seq, LANES), kvmap),
                  pl.BlockSpec((None, None, tq, 2), lambda b, p, i: (b, p, i, 0)),
                  pl.BlockSpec((None, None, 2, seq), lambda b, p, i: (b, p, 0, 0))],
        out_specs=pl.BlockSpec((tq, LANES), qmap),
        out_shape=jax.ShapeDtypeStruct(q.shape, BF16),
        compiler_params=_cparams(("parallel", "parallel", "arbitrary")),
        name="fox_attention",
    )(q, k, v, ccol, crow)


def _diff_block(c, lam_init, q_ref, k_ref, v_ref, lq1_ref, lk1_ref, lq2_ref, lk2_ref, sub_ref, o_ref):
    tq = ATT_TQ
    nk = (c + 1) * tq
    q_st = _stack_heads(q_ref[...])

    def score_chunk(k0, k1):
        s = _nt_dot(q_st, k_ref[k0:k1, :])
        return _fill_diag(s, tq, False, -jnp.inf) if k1 == nk else s

    o = _online_softmax_pv(2 * tq, nk, score_chunk, v_ref)
    lam =(jnp.exp(jnp.sum(lq1_ref[...] * lk1_ref[...], axis=-1, keepdims=True))
           - jnp.exp(jnp.sum(lq2_ref[...] * lk2_ref[...], axis=-1, keepdims=True)) + lam_init)
    o = o[:tq] - lam * o[tq:]
    o_ref[...] = (_rms_rows(o, sub_ref[...]) * (1.0 - lam_init)).astype(o_ref.dtype)


def _diff_body(lam_init, *refs):
    _per_query_block(pl.program_id(2), refs[1].shape[0] // ATT_TQ, lambda c: _diff_block(c, lam_init, *refs))


def _diff_attention(q, k, v, lq1, lk1, lq2, lk2, subln, lam_init, batch, seq):
    tq = ATT_TQ
    nq = seq // tq
    heads = q.shape[1] // LANES
    qmap = lambda b, h, i: (b * nq + i, h)
    kvmap = lambda b, h, i: (b, h)
    vec = lambda n: pl.BlockSpec((1, n), lambda b, h, i: (0, 0))
    return pl.pallas_call(
        functools.partial(_diff_body, lam_init),
        grid=(batch, heads, nq),
        in_specs=[pl.BlockSpec((tq, LANES), qmap),
                  pl.BlockSpec((seq, LANES), kvmap),
                  pl.BlockSpec((seq, LANES), kvmap),
                  vec(HEAD_DIM), vec(HEAD_DIM), vec(HEAD_DIM), vec(HEAD_DIM), vec(LANES)],
        out_specs=pl.BlockSpec((tq, LANES), qmap),
        out_shape=jax.ShapeDtypeStruct(q.shape, BF16),
        compiler_params=_cparams(("parallel", "parallel", "arbitrary")),
        name="diff_attention",
    )(q, k, v, lq1.reshape(1, -1), lk1.reshape(1, -1), lq2.reshape(1, -1), lk2.reshape(1, -1),
      subln.reshape(1, -1))


def _sb_block(c, q_ref, k_ref, v_ref, o_ref):
    tq = tk = ATT_TQ
    nk = (c + 1) * tq
    z = _nt_dot(_stack_heads(q_ref[...]), k_ref[0:nk, :])
    log1m = -(jnp.maximum(z, 0.0) + jnp.log1p(jnp.exp(-jnp.abs(z))))
    log1m = _fill_diag(log1m, tq, True, 0.0)
    ri = lax.broadcasted_iota(I32, (2 * tk, tk), 0) & (tk - 1)
    ci = lax.broadcasted_iota(I32, (2 * tk, tk), 1)
    later = jnp.where(ri > ci, 1.0, 0.0).astype(BF16)
    run = jnp.zeros((2 * tq, 1), F32)
    acc = jnp.zeros((2 * tq, LANES), F32)
    for blk in reversed(range(nk // tk)):
        lb = log1m[:, blk * tk:(blk + 1) * tk]
        hi = lb.astype(BF16)
        lo = (lb - hi.astype(F32)).astype(BF16)
        local = jnp.dot(jnp.concatenate([hi, lo], axis=1), later, preferred_element_type=F32)
        a = jnp.exp(z[:, blk * tk:(blk + 1) * tk] + lb + (local + run))
        if blk == c:
            a = jnp.where(_diag_mask(2 * tq, tq, True), a, 0.0)
        acc = acc + jnp.dot(a.astype(BF16), v_ref[blk * tk:(blk + 1) * tk, :], preferred_element_type=F32)
        run = run + (local[:, 0:1] + lb[:, 0:1])
    o_ref[...] = _unstack_heads(acc).astype(o_ref.dtype)


def _sb_body(*refs):
    _per_query_block(pl.program_id(2), refs[1].shape[0] // ATT_TQ, lambda c: _sb_block(c, *refs))


def _sb_attention(q, k, v, batch, seq):
    tq = ATT_TQ
    nq = seq // tq
    pairs = q.shape[1] // LANES
    qmap = lambda b, p, i: (b * nq + i, p)
    kvmap = lambda b, p, i: (b, p)
    return pl.pallas_call(
        _sb_body,
        grid=(batch, pairs, nq),
        in_specs=[pl.BlockSpec((tq, LANES), qmap),
                  pl.BlockSpec((seq, LANES), kvmap),
                  pl.BlockSpec((seq, LANES), kvmap)],
        out_specs=pl.BlockSpec((tq, LANES), qmap),
        out_shape=jax.ShapeDtypeStruct(q.shape, BF16),
        compiler_params=_cparams(("parallel", "parallel", "arbitrary")),
        name="stickbreak_attention",
    )(q, k, v)


def _count(pred):
    return jnp.sum(jnp.where(pred, 1.0, 0.0), axis=-1, keepdims=True)


def _dsa_block(c, top_k, iq_ref, iw_ref, cq_ref, ika_ref, ikb_ref, kv_ref, mq_ref, wuv_ref,
               o_ref, key_scr, sel_scr, cut_scr):
    tq = DSA_TQ
    edge = DSA_STEPS_PER_BRANCH * tq
    nk = (c + 1) * edge
    q_in_edge = (pl.program_id(1) % DSA_STEPS_PER_BRANCH) * tq + lax.broadcasted_iota(I32, (tq, edge), 0)
    causal = lax.broadcasted_iota(I32, (tq, edge), 1) <= q_in_edge

    def on_edge(x, fn):
        return jnp.concatenate([x[:, :nk - edge], fn(x[:, nk - edge:])], axis=1) if nk > edge else fn(x)

    iq = iq_ref[...]
    iq_st = jnp.concatenate([iq[:, :LANES], iq[:, LANES:]], axis=0)
    la = jnp.maximum(_nt_dot(iq_st, ika_ref[0:nk, :]), 0.0)
    lb = jnp.maximum(_nt_dot(iq_st, ikb_ref[0:nk, :]), 0.0)
    score = (la[:tq] * iw_ref[:, 0:1] + lb[:tq] * iw_ref[:, 1:2]
             + la[tq:] * iw_ref[:, 2:3] + lb[tq:] * iw_ref[:, 3:4])

    bits = pltpu.bitcast(score, I32)
    keys = bits ^ ((bits >> 31) & jnp.int32(0x7FFFFFFF))
    keys = jnp.where(bits == jnp.int32(INT_MIN), 0, keys)
    key_scr[:, 0:nk] = on_edge(keys, lambda x: jnp.where(causal, x, jnp.int32(INT_MIN)))

    def value_bit(i, tu):
        cand = tu | lax.shift_left(jnp.int32(1), 31 - i)
        return jnp.where(_count(key_scr[:, 0:nk] >= (cand ^ jnp.int32(INT_MIN))) >= top_k, cand, tu)

    thr = lax.fori_loop(0, 32, value_bit, jnp.zeros((tq, 1), I32)) ^ jnp.int32(INT_MIN)
    keys = key_scr[:, 0:nk]
    above = keys > thr
    need = top_k - _count(above)

    excess = jnp.where(thr == jnp.int32(INT_MIN), 0.0, _count(keys == thr) - need)
    cut_scr[...] = jnp.full(cut_scr.shape, nk, I32)

    @pl.when(jnp.max(excess) > 0.0)
    def _():
        kpos = lax.broadcasted_iota(I32, (tq, nk), 1)
        index_bits = (nk - 1).bit_length()

        def index_bit(i, x):
            cand = x | lax.shift_left(jnp.int32(1), index_bits - 1 - i)
            return jnp.where(_count((key_scr[:, 0:nk] == thr) & (kpos < cand)) < need, cand, x)

        cut = lax.fori_loop(0, index_bits, index_bit, jnp.zeros((tq, 1), I32))
        cut_scr[...] = jnp.broadcast_to(cut, cut_scr.shape)

    kpos = lax.broadcasted_iota(I32, (tq, nk), 1)
    sel = jnp.where(above | ((keys == thr) & (kpos <= cut_scr[:, 0:1])), 0.0, -jnp.inf)
    sel_scr[:, 0:nk] = on_edge(sel, lambda x: jnp.where(causal, x, -jnp.inf))

    def head_pair(p, carry):
        t = jnp.dot(cq_ref[p], mq_ref[p], preferred_element_type=F32)
        qcat = jnp.concatenate([t[:, :2 * LANES], t[:, 2 * LANES:]], axis=0).astype(BF16)
        s = _nt_dot(qcat, kv_ref[0:nk, :])
        sel = sel_scr[:, 0:nk]
        s = jnp.concatenate([s[:tq] + sel, s[tq:] + sel], axis=0)
        o_lat = _softmax_pv(s, kv_ref[0:nk, 0:DSA_KV_RANK]).astype(BF16)
        pair = jnp.concatenate([o_lat[:tq], o_lat[tq:]], axis=1)
        o_ref[p] = jnp.dot(pair, wuv_ref[p], preferred_element_type=F32).astype(o_ref.dtype)
        return carry

    lax.fori_loop(0, DSA_HEADS // 2, head_pair, 0)


def _dsa_body(top_k, *refs):
    kv_ref = refs[5]
    _per_query_block(pl.program_id(1) // DSA_STEPS_PER_BRANCH, kv_ref.shape[0] // (DSA_STEPS_PER_BRANCH * DSA_TQ),
                     lambda c: _dsa_block(c, top_k, *refs))


def _dsa_attention(iq, iw, cq, ika, ikb, kv, mq, wuv, top_k, batch, seq):
    tq = DSA_TQ
    nq = seq // tq
    qmap = lambda b, i: (b * nq + i, 0)
    pmap = lambda b, i: (0, b * nq + i, 0)
    kmap = lambda b, i: (b, 0)
    cmap = lambda b, i: (0, 0, 0)
    pairs = cq.shape[0]
    return pl.pallas_call(
        functools.partial(_dsa_body, top_k),
        grid=(batch, nq),
        in_specs=[pl.BlockSpec((tq, iq.shape[1]), qmap),
                  pl.BlockSpec((tq, LANES), qmap),
                  pl.BlockSpec((pairs, tq, LANES), pmap),
                  pl.BlockSpec((seq, LANES), kmap),
                  pl.BlockSpec((seq, LANES), kmap),
                  pl.BlockSpec((seq, 2 * LANES), kmap),
                  pl.BlockSpec(mq.shape, cmap),
                  pl.BlockSpec(wuv.shape, cmap)],
        out_specs=pl.BlockSpec((pairs, tq, LANES), pmap),
        out_shape=jax.ShapeDtypeStruct(cq.shape, BF16),
        scratch_shapes=[pltpu.VMEM((tq, seq), I32), pltpu.VMEM((tq, seq), F32), pltpu.VMEM((tq, LANES), I32)],
        compiler_params=_cparams(("parallel", "arbitrary")),
        name="dsa_attention",
    )(iq, iw, cq, ika, ikb, kv, mq, wuv)


def _rope_tables(positions):
    inv_freq = ROPE_THETA ** (-jnp.arange(0, ROT_DIM, 2, dtype=F32) / ROT_DIM)
    ang = positions.astype(F32).reshape(-1, 1) * inv_freq
    cos, sin = jnp.cos(ang), jnp.sin(ang)
    t = ang.shape[0]
    pad = jnp.zeros((t, HEAD_DIM - ROT_DIM), F32)
    zero8 = jnp.zeros((t, ROT_HALF), F32)
    c64 = jnp.concatenate([cos, cos, pad + 1.0], axis=1)
    sa64 = jnp.concatenate([-sin, zero8, pad], axis=1)
    sb64 = jnp.concatenate([zero8, sin, pad], axis=1)
    return tuple(jnp.tile(a, (1, LANES // HEAD_DIM)) for a in (c64, sa64, sb64))


def _pad_cols(w, width):
    return jnp.pad(w, ((0, 0), (0, width - w.shape[1])))


EVEN_GROUPS = ((FOX_WIDTH, "plain", False), (FOX_WIDTH, "plain", False), (FOX_WIDTH, "plain", False),
               (DIFF_WIDTH, "rope", False), (DIFF_WIDTH, "rope", False), (DIFF_WIDTH, "plain", False),
               (LANES, "f32", False))
ODD_GROUPS = ((DSA_WIDTH, "rope", True), (IDX_HEADS * IDX_DIM, "rope", False), (LANES, "rope", False),
              (LANES, "rope", False), (2 * LANES, "kv", False),
              (SB_WIDTH, "plain", False), (SB_WIDTH, "plain", False), (SB_WIDTH, "plain", False),
              (LANES, "f32", False))


def _pack_even(w):
    aq, ak, av, af, bq, bk, bv = jnp.split(w, (512, 1024, 1536, 1544, 2056, 2568), axis=1)
    return jnp.concatenate([aq * QK_SCALE, ak, av, bq * QK_SCALE, bk, bv, _pad_cols(af, LANES)],
                           axis=1).astype(BF16)


def _pack_odd(w):
    cq, ckv, ckr, ciq, ciw, cik, sq, sk, sv = jnp.split(w, (512, 640, 656, 912, 916, 980, 1492, 2004), axis=1)
    zeros64 = jnp.zeros_like(cik)
    return jnp.concatenate([cq * QK_SCALE, ciq,
                            jnp.concatenate([cik, zeros64], axis=1), jnp.concatenate([zeros64, cik], axis=1),
                            ckv, _pad_cols(ckr, LANES),
                            sq * QK_SCALE, sk, sv, _pad_cols(ciw * (IDX_HEADS ** -0.5), LANES)],
                           axis=1).astype(BF16)


def _pack_dsa_weights(w_uk, w_uv):
    mq, wuv = [], []
    eye = jnp.eye(ROT_DIM, dtype=F32)
    for p in range(DSA_HEADS // 2):
        m = jnp.zeros((LANES, 4 * LANES), F32)
        u = jnp.zeros((2 * DSA_KV_RANK, LANES), F32)
        for hh in range(2):
            h = 2 * p + hh
            r0 = hh * HEAD_DIM
            c0 = hh * 2 * LANES
            m = m.at[r0 + ROT_DIM:r0 + HEAD_DIM, c0:c0 + DSA_KV_RANK].set(w_uk[:, h, :].T)
            m = m.at[r0:r0 + ROT_DIM, c0 + DSA_KV_RANK:c0 + DSA_KV_RANK + ROT_DIM].set(eye)
            u = u.at[hh * DSA_KV_RANK:(hh + 1) * DSA_KV_RANK, r0:r0 + HEAD_DIM].set(w_uv[:, h, :])
        mq.append(m)
        wuv.append(u)
    return jnp.stack(mq).astype(BF16), jnp.stack(wuv).astype(BF16)


def kernel(x, positions, attn_norm, ffn_norm, final_norm, ev_w_in, ev_fgate_b, ev_lambda_q1, ev_lambda_k1,
           ev_lambda_q2, ev_lambda_k2, ev_subln, ev_w_out, od_w_in, od_kv_norm, od_w_uk, od_w_uv, od_w_out,
           ffn_w_in, ffn_conv_w, ffn_conv_b, ffn_w_out):
    batch, seq, _ = x.shape
    depth = attn_norm.shape[0]
    t = batch * seq
    top_k = min(IDX_TOPK_MAX, seq // 4)
    tabs = _rope_tables(positions)
    xf = x.reshape(t, D_MODEL)
    ones_kv = jnp.ones((LANES,), F32)
    for layer in range(depth):
        j = layer // 2
        if layer % 2 == 0:
            aq, ak, av, bq, bk, bv, af = _in_proj(xf, attn_norm[layer], _pack_even(ev_w_in[j]), tabs, ones_kv,
                                                  EVEN_GROUPS)
            c = _fgate_cumsum(af, _pad_cols(ev_fgate_b[j].reshape(1, -1), LANES), batch, seq)
            c8 = c.reshape(batch, seq, LANES)[:, :, :FOX_WIDTH // HEAD_DIM]
            ccol = c8.reshape(batch, seq, -1, 2).transpose(0, 2, 1, 3)
            crow = c8.reshape(batch, seq, -1, 2).transpose(0, 2, 3, 1)
            o_a = _fox_attention(aq, ak, av, ccol, crow, batch, seq)
            lam_init = 0.8 - 0.6 * math.exp(-0.3 * layer)
            o_b = _diff_attention(bq, bk, bv, ev_lambda_q1[j], ev_lambda_k1[j], ev_lambda_q2[j], ev_lambda_k2[j],
                                  ev_subln[j], lam_init, batch, seq)
            xf = _out_proj(xf, o_a, o_b, ev_w_out[j].astype(BF16))
        else:
            cq, iq, ika, ikb, kv, sq, sk, sv, iw = _in_proj(xf, attn_norm[layer], _pack_odd(od_w_in[j]), tabs,
                                                            od_kv_norm[j], ODD_GROUPS)
            mq, wuv = _pack_dsa_weights(od_w_uk[j], od_w_uv[j])
            o_c = _dsa_attention(iq, iw, cq, ika, ikb, kv, mq, wuv, top_k, batch, seq)
            o_d = _sb_attention(sq, sk, sv, batch, seq)
            xf = _out_proj(xf, o_c, o_d, od_w_out[j].astype(BF16))
        xf = _ffn(xf, ffn_norm[layer], ffn_w_in[layer].astype(BF16), ffn_conv_w[layer], ffn_conv_b[layer],
                  ffn_w_out[layer].astype(BF16), final_norm, seq, layer == depth - 1)
    return xf.reshape(batch, seq, D_MODEL)
```

```python
import functools
import math

import jax
import jax.numpy as jnp
from jax import lax
from jax.experimental import pallas as pl
from jax.experimental.pallas import tpu as pltpu

F32 = jnp.float32
BF16 = jnp.bfloat16
I32 = jnp.int32

D_MODEL = 1024
HEAD_DIM = 64
ROT_DIM = 16
ROT_HALF = 8
ROPE_THETA = 500000.0
EPS = 1e-6
LANES = 128
QK_SCALE = HEAD_DIM ** -0.5
FOX_WIDTH = 512
DIFF_WIDTH = 512
DIFF_HEADS = 4
DSA_WIDTH = 512
DSA_HEADS = 8
DSA_NOPE = HEAD_DIM - ROT_DIM
DSA_KV_RANK = 128
IDX_HEADS = 4
IDX_DIM = 64
IDX_TOPK_MAX = 256
SB_WIDTH = 512
D_FF = 2816
INT_MIN = -(2 ** 31)

VMEM_LIMIT = 56 * 1024 * 1024

PROJ_TM = 512
PROJ_CHUNK = 256
FFN_TM = 512
FFN_HALO = 16
FFN_CHUNK = 256
ATT_TQ = 256
ATT_CHUNK = 512
DSA_TQ = 256
DSA_KC = 256


def _cparams(sem):
    return pltpu.CompilerParams(dimension_semantics=sem, vmem_limit_bytes=VMEM_LIMIT)


def _rms_rows(x, g):
    ms = jnp.mean(x * x, axis=-1, keepdims=True)
    return x * lax.rsqrt(ms + EPS) * g


def _rope128(z, c, sa, sb):
    return z * c + pltpu.roll(z, LANES - ROT_HALF, axis=1) * sa + pltpu.roll(z, ROT_HALF, axis=1) * sb


def _in_proj_body(groups, x_ref, g_ref, w_ref, c_ref, sa_ref, sb_ref, kvg_ref, *rest):
    out_refs, h_scr = rest[:-1], rest[-1]
    h_scr[...] = _rms_rows(x_ref[...], g_ref[...]).astype(BF16)
    col = 0
    for (width, kind, block_major), o_ref in zip(groups, out_refs):
        cw = min(width, PROJ_CHUNK)
        for c0 in range(0, width, cw):
            zc = jnp.dot(h_scr[...], w_ref[:, col + c0:col + c0 + cw], preferred_element_type=F32)
            for l0 in range(0, cw, LANES):
                z = zc[:, l0:l0 + LANES]
                if kind == "rope" or (kind == "kv" and c0 + l0 == DSA_KV_RANK):
                    z = _rope128(z, c_ref[...], sa_ref[...], sb_ref[...])
                elif kind == "kv":
                    z = _rms_rows(z, kvg_ref[...])
                if block_major:
                    o_ref[(c0 + l0) // LANES] = z.astype(o_ref.dtype)
                else:
                    o_ref[:, c0 + l0:c0 + l0 + LANES] = z.astype(o_ref.dtype)
        col += width


def _in_proj(x, gain, w_cat, tabs, kv_gain, groups):
    t = x.shape[0]
    n = w_cat.shape[1]
    tm = PROJ_TM
    row = lambda i: (i, 0)
    const = lambda i: (0, 0)
    out_shape, out_specs = [], []
    for w, kind, block_major in groups:
        dtype = F32 if kind == "f32" else BF16
        if block_major:
            out_shape.append(jax.ShapeDtypeStruct((w // LANES, t, LANES), dtype))
            out_specs.append(pl.BlockSpec((w // LANES, tm, LANES), lambda i: (0, i, 0)))
        else:
            out_shape.append(jax.ShapeDtypeStruct((t, w), dtype))
            out_specs.append(pl.BlockSpec((tm, w), row))
    return pl.pallas_call(
        functools.partial(_in_proj_body, groups),
        grid=(t // tm,),
        in_specs=[pl.BlockSpec((tm, D_MODEL), row),
                  pl.BlockSpec((1, D_MODEL), const),
                  pl.BlockSpec((D_MODEL, n), const),
                  pl.BlockSpec((tm, LANES), row),
                  pl.BlockSpec((tm, LANES), row),
                  pl.BlockSpec((tm, LANES), row),
                  pl.BlockSpec((1, LANES), const)],
        out_specs=out_specs,
        out_shape=out_shape,
        scratch_shapes=[pltpu.VMEM((tm, D_MODEL), BF16)],
        compiler_params=_cparams(("parallel",)),
        name="in_proj",
    )(x, gain.reshape(1, D_MODEL), w_cat, tabs[0], tabs[1], tabs[2], kv_gain.reshape(1, LANES))


def _out_proj_body(x_ref, a_ref, b_ref, w_ref, o_ref):
    if len(a_ref.shape) == 3:
        a = jnp.concatenate([a_ref[i] for i in range(a_ref.shape[0])], axis=1)
    else:
        a = a_ref[...]
    half = a.shape[1]
    y = jnp.dot(a, w_ref[:half, :], preferred_element_type=F32)
    y = y + jnp.dot(b_ref[...], w_ref[half:, :], preferred_element_type=F32)
    o_ref[...] = x_ref[...] + y


def _out_proj(x, a, b, w):
    t = x.shape[0]
    tm = PROJ_TM
    row = lambda i: (i, 0)
    const = lambda i: (0, 0)
    a_spec = (pl.BlockSpec((a.shape[0], tm, LANES), lambda i: (0, i, 0)) if a.ndim == 3
              else pl.BlockSpec((tm, a.shape[1]), row))
    return pl.pallas_call(
        _out_proj_body,
        grid=(t // tm,),
        in_specs=[pl.BlockSpec((tm, D_MODEL), row),
                  a_spec,
                  pl.BlockSpec((tm, b.shape[1]), row),
                  pl.BlockSpec((D_MODEL, D_MODEL), const)],
        out_specs=pl.BlockSpec((tm, D_MODEL), row),
        out_shape=jax.ShapeDtypeStruct((t, D_MODEL), F32),
        compiler_params=_cparams(("parallel",)),
        name="out_proj",
    )(x, a, b, w)


def _ffn_body(tiles_per_seq, final, x_ref, xh_ref, g_ref, win_ref, cw_ref, cb_ref, wout_ref, fg_ref,
              o_ref, h_scr, act_scr):
    tm = x_ref.shape[0]
    x = x_ref[...]
    g = g_ref[...]
    keep = (pl.program_id(0) % tiles_per_seq != 0).astype(F32)
    h_scr[:FFN_HALO, :] = (_rms_rows(xh_ref[...], g) * keep).astype(BF16)
    h_scr[FFN_HALO:, :] = _rms_rows(x, g).astype(BF16)

    def conv(u, c0):
        y = (pltpu.roll(u, 2, axis=0) * cw_ref[0:1, c0:c0 + FFN_CHUNK]
             + pltpu.roll(u, 1, axis=0) * cw_ref[1:2, c0:c0 + FFN_CHUNK]
             + u * cw_ref[2:3, c0:c0 + FFN_CHUNK])
        return y[FFN_HALO:, :] + cb_ref[0:1, c0:c0 + FFN_CHUNK]

    for ci in range(D_FF // FFN_CHUNK):
        cg = ci * FFN_CHUNK
        cv = D_FF + cg
        ug = jnp.dot(h_scr[...], win_ref[:, cg:cg + FFN_CHUNK], preferred_element_type=F32)
        uv = jnp.dot(h_scr[...], win_ref[:, cv:cv + FFN_CHUNK], preferred_element_type=F32)
        gate = conv(ug, cg)
        val = conv(uv, cv)
        act_scr[:, cg:cg + FFN_CHUNK] = (gate * jax.nn.sigmoid(gate) * val).astype(BF16)
    y = x + jnp.dot(act_scr[...], wout_ref[...], preferred_element_type=F32)
    if final:
        y = _rms_rows(y, fg_ref[...])
    o_ref[...] = y


def _ffn(x, gain, w_in, conv_w, conv_b, w_out, final_gain, seq, final):
    t = x.shape[0]
    tm = FFN_TM
    halo_blocks = tm // FFN_HALO
    row = lambda i: (i, 0)
    const = lambda i: (0, 0)
    return pl.pallas_call(
        functools.partial(_ffn_body, seq // tm, final),
        grid=(t // tm,),
        in_specs=[pl.BlockSpec((tm, D_MODEL), row),
                  pl.BlockSpec((FFN_HALO, D_MODEL), lambda i: (jnp.maximum(i * halo_blocks - 1, 0), 0)),
                  pl.BlockSpec((1, D_MODEL), const),
                  pl.BlockSpec((D_MODEL, 2 * D_FF), const),
                  pl.BlockSpec((3, 2 * D_FF), const),
                  pl.BlockSpec((1, 2 * D_FF), const),
                  pl.BlockSpec((D_FF, D_MODEL), const),
                  pl.BlockSpec((1, D_MODEL), const)],
        out_specs=pl.BlockSpec((tm, D_MODEL), row),
        out_shape=jax.ShapeDtypeStruct((t, D_MODEL), F32),
        scratch_shapes=[pltpu.VMEM((tm + FFN_HALO, D_MODEL), BF16), pltpu.VMEM((tm, D_FF), BF16)],
        compiler_params=_cparams(("parallel",)),
        name="conv_ffn",
    )(x, x, gain.reshape(1, D_MODEL), w_in, conv_w, conv_b.reshape(1, 2 * D_FF), w_out,
      final_gain.reshape(1, D_MODEL))


def _split3(x):
    hi = x.astype(BF16)
    r = x - hi.astype(F32)
    mid = r.astype(BF16)
    lo = (r - mid.astype(F32)).astype(BF16)
    return hi, mid, lo


def _fgate_body(af_ref, b_ref, c_ref):
    s = af_ref.shape[0]
    blk = 512
    ri = lax.broadcasted_iota(I32, (blk, blk), 0)
    ci = lax.broadcasted_iota(I32, (blk, blk), 1)
    tri = jnp.where(ci <= ri, 1.0, 0.0).astype(BF16)
    carry = jnp.zeros((1, LANES), F32)
    for r0 in range(0, s, blk):
        lf = jax.nn.log_sigmoid(af_ref[r0:r0 + blk, :] + b_ref[...])
        hi, mid, lo = _split3(lf)
        c = (jnp.dot(tri, hi, preferred_element_type=F32)
             + jnp.dot(tri, mid, preferred_element_type=F32)
             + jnp.dot(tri, lo, preferred_element_type=F32)) + carry
        c_ref[r0:r0 + blk, :] = c
        carry = c[blk - 1:blk, :]


def _fgate_cumsum(af, bias, batch, seq):
    return pl.pallas_call(
        _fgate_body,
        grid=(batch,),
        in_specs=[pl.BlockSpec((seq, LANES), lambda b: (b, 0)),
                  pl.BlockSpec((1, LANES), lambda b: (0, 0))],
        out_specs=pl.BlockSpec((seq, LANES), lambda b: (b, 0)),
        out_shape=jax.ShapeDtypeStruct((batch * seq, LANES), F32),
        compiler_params=_cparams(("parallel",)),
        name="fox_gate_cumsum",
    )(af, bias)


def _nt_dot(a, b):
    return lax.dot_general(a, b, (((1,), (1,)), ((), ())), preferred_element_type=F32)


def _per_query_block(qi, n, fn):
    for c in range(n):
        @pl.when(qi == c)
        def _():
            fn(c)


def _stack_heads(q):
    lane = lax.broadcasted_iota(I32, (1, LANES), 1)
    zero = jnp.zeros_like(q)
    return jnp.concatenate([jnp.where(lane < HEAD_DIM, q, zero), jnp.where(lane >= HEAD_DIM, q, zero)], axis=0)


def _unstack_heads(o):
    tq = o.shape[0] // 2
    lane = lax.broadcasted_iota(I32, (1, LANES), 1)
    return jnp.where(lane < HEAD_DIM, o[:tq], o[tq:])


def _diag_mask(rows, tq, strict):
    qpos = lax.broadcasted_iota(I32, (rows, tq), 0) & (tq - 1)
    kpos = lax.broadcasted_iota(I32, (rows, tq), 1)
    return kpos < qpos if strict else kpos <= qpos


def _fill_diag(s, tq, strict, fill):
    nk = s.shape[1]
    d = jnp.where(_diag_mask(s.shape[0], tq, strict), s[:, nk - tq:], fill)
    return d if nk == tq else jnp.concatenate([s[:, :nk - tq], d], axis=1)


def _softmax_pv(s, v):
    m = jnp.max(s, axis=-1, keepdims=True)
    e = jnp.exp(s - m)
    l = jnp.sum(e, axis=-1, keepdims=True)
    return jnp.dot(e.astype(BF16), v, preferred_element_type=F32) / l


def _online_softmax_pv(rows, nk, score_chunk, v_ref):
    m = jnp.full((rows, 1), -jnp.inf, F32)
    l = jnp.zeros((rows, 1), F32)
    acc = jnp.zeros((rows, LANES), F32)
    for k0 in range(0, nk, ATT_CHUNK):
        k1 = min(k0 + ATT_CHUNK, nk)
        s = score_chunk(k0, k1)
        m_new = jnp.maximum(m, jnp.max(s, axis=-1, keepdims=True))
        p = jnp.exp(s - m_new)
        alpha = jnp.exp(m - m_new)
        l = alpha * l + jnp.sum(p, axis=-1, keepdims=True)
        acc = alpha * acc + jnp.dot(p.astype(BF16), v_ref[k0:k1, :], preferred_element_type=F32)
        m = m_new
    return acc / l


def _fox_block(c, q_ref, k_ref, v_ref, ccol_ref, crow_ref, o_ref):
    tq = ATT_TQ
    nk = (c + 1) * tq
    q_st = _stack_heads(q_ref[...])

    def score_chunk(k0, k1):
        s = _nt_dot(q_st, k_ref[k0:k1, :])
        bias = jnp.concatenate([ccol_ref[:, 0:1] - crow_ref[0:1, k0:k1],
                                ccol_ref[:, 1:2] - crow_ref[1:2, k0:k1]], axis=0)
        return _fill_diag(s + bias, tq, False, -jnp.inf) if k1 == nk else s + bias

    o_ref[...] = _unstack_heads(_online_softmax_pv(2 * tq, nk, score_chunk, v_ref)).astype(o_ref.dtype)


def _fox_body(*refs):
    _per_query_block(pl.program_id(2), refs[1].shape[0] // ATT_TQ, lambda c: _fox_block(c, *refs))


def _fox_attention(q, k, v, ccol, crow, batch, seq):
    tq = ATT_TQ
    nq = seq // tq
    pairs = q.shape[1] // LANES
    qmap = lambda b, p, i: (b * nq + i, p)
    kvmap = lambda b, p, i: (b, p)
    return pl.pallas_call(
        _fox_body,
        grid=(batch, pairs, nq),
        in_specs=[pl.BlockSpec((tq, LANES), qmap),
                  pl.BlockSpec((seq, LANES), kvmap),
                  pl.BlockSpec((seq, LANES), kvmap),
                  pl.BlockSpec((None, None, tq, 2), lambda b, p, i: (b, p, i, 0)),
                  pl.BlockSpec((None, None, 2, seq), lambda b, p, i: (b, p, 0, 0))],
        out_specs=pl.BlockSpec((tq, LANES), qmap),
        out_shape=jax.ShapeDtypeStruct(q.shape, BF16),
        compiler_params=_cparams(("parallel", "parallel", "arbitrary")),
        name="fox_attention",
    )(q, k, v, ccol, crow)


def _diff_block(c, lam_init, q_ref, k_ref, v_ref, lq1_ref, lk1_ref, lq2_ref, lk2_ref, sub_ref, o_ref):
    tq = ATT_TQ
    nk = (c + 1) * tq
    q_st = _stack_heads(q_ref[...])

    def score_chunk(k0, k1):
        s = _nt_dot(q_st, k_ref[k0:k1, :])
        return _fill_diag(s, tq, False, -jnp.inf) if k1 == nk else s

    o = _online_softmax_pv(2 * tq, nk, score_chunk, v_ref)
    lam =(jnp.exp(jnp.sum(lq1_ref[...] * lk1_ref[...], axis=-1, keepdims=True))
           - jnp.exp(jnp.sum(lq2_ref[...] * lk2_ref[...], axis=-1, keepdims=True)) + lam_init)
    o = o[:tq] - lam * o[tq:]
    o_ref[...] = (_rms_rows(o, sub_ref[...]) * (1.0 - lam_init)).astype(o_ref.dtype)


def _diff_body(lam_init, *refs):
    _per_query_block(pl.program_id(2), refs[1].shape[0] // ATT_TQ, lambda c: _diff_block(c, lam_init, *refs))


def _diff_attention(q, k, v, lq1, lk1, lq2, lk2, subln, lam_init, batch, seq):
    tq = ATT_TQ
    nq = seq // tq
    heads = q.shape[1] // LANES
    qmap = lambda b, h, i: (b * nq + i, h)
    kvmap = lambda b, h, i: (b, h)
    vec = lambda n: pl.BlockSpec((1, n), lambda b, h, i: (0, 0))
    return pl.pallas_call(
        functools.partial(_diff_body, lam_init),
        grid=(batch, heads, nq),
        in_specs=[pl.BlockSpec((tq, LANES), qmap),
                  pl.BlockSpec((seq, LANES), kvmap),
                  pl.BlockSpec((seq, LANES), kvmap),
                  vec(HEAD_DIM), vec(HEAD_DIM), vec(HEAD_DIM), vec(HEAD_DIM), vec(LANES)],
        out_specs=pl.BlockSpec((tq, LANES), qmap),
        out_shape=jax.ShapeDtypeStruct(q.shape, BF16),
        compiler_params=_cparams(("parallel", "parallel", "arbitrary")),
        name="diff_attention",
    )(q, k, v, lq1.reshape(1, -1), lk1.reshape(1, -1), lq2.reshape(1, -1), lk2.reshape(1, -1),
      subln.reshape(1, -1))


def _sb_block(c, q_ref, k_ref, v_ref, o_ref):
    tq = tk = ATT_TQ
    nk = (c + 1) * tq
    z = _nt_dot(_stack_heads(q_ref[...]), k_ref[0:nk, :])
    log1m = -jnp.maximum(z, 0.0) - jnp.log(1.0 + jnp.exp(-jnp.abs(z)))
    log1m = _fill_diag(log1m, tq, True, 0.0)
    ri = lax.broadcasted_iota(I32, (2 * tk, tk), 0) & (tk - 1)
    ci = lax.broadcasted_iota(I32, (2 * tk, tk), 1)
    later = jnp.where(ri > ci, 1.0, 0.0).astype(BF16)
    run = jnp.zeros((2 * tq, 1), F32)
    acc = jnp.zeros((2 * tq, LANES), F32)
    for blk in reversed(range(nk // tk)):
        lb = log1m[:, blk * tk:(blk + 1) * tk]
        hi = lb.astype(BF16)
        lo = (lb - hi.astype(F32)).astype(BF16)
        local = jnp.dot(jnp.concatenate([hi, lo], axis=1), later, preferred_element_type=F32)
        a = jnp.exp(z[:, blk * tk:(blk + 1) * tk] + lb + (local + run))
        if blk == c:
            a = jnp.where(_diag_mask(2 * tq, tq, True), a, 0.0)
        acc = acc + jnp.dot(a.astype(BF16), v_ref[blk * tk:(blk + 1) * tk, :], preferred_element_type=F32)
        run = run + (local[:, 0:1] + lb[:, 0:1])
    o_ref[...] = _unstack_heads(acc).astype(o_ref.dtype)


def _sb_body(*refs):
    _per_query_block(pl.program_id(2), refs[1].shape[0] // ATT_TQ, lambda c: _sb_block(c, *refs))


def _sb_attention(q, k, v, batch, seq):
    tq = ATT_TQ
    nq = seq // tq
    pairs = q.shape[1] // LANES
    qmap = lambda b, p, i: (b * nq + i, p)
    kvmap = lambda b, p, i: (b, p)
    return pl.pallas_call(
        _sb_body,
        grid=(batch, pairs, nq),
        in_specs=[pl.BlockSpec((tq, LANES), qmap),
                  pl.BlockSpec((seq, LANES), kvmap),
                  pl.BlockSpec((seq, LANES), kvmap)],
        out_specs=pl.BlockSpec((tq, LANES), qmap),
        out_shape=jax.ShapeDtypeStruct(q.shape, BF16),
        compiler_params=_cparams(("parallel", "parallel", "arbitrary")),
        name="stickbreak_attention",
    )(q, k, v)


def _fold_rows(x, op):
    while x.shape[0] > 8:
        half = x.shape[0] // 2
        x = op(x[:half], x[half:])
    return x


def _dsa_body(top_k, iq_ref, iwt_ref, cq_ref, ika_ref, ikb_ref, kv_ref, ckvt_ref, mq_ref, wuvt_ref,
              o_ref, key_scr, sel_scr, qcat_scr, cut_scr, m_scr, l_scr, acc_scr):
    tq, kc = DSA_TQ, DSA_KC
    qi = pl.program_id(1)
    n_chunks = qi + 1
    int_min = jnp.int32(INT_MIN)
    qpos = qi * tq + lax.broadcasted_iota(I32, (kc, tq), 1)
    krow = lax.broadcasted_iota(I32, (kc, tq), 0)

    def over_chunks(fn, init):
        return lax.fori_loop(0, n_chunks, fn, init)

    def count_keys(pred_of_chunk):
        def add(j, acc):
            ones = jnp.where(pred_of_chunk(j, key_scr[j]), 1.0, 0.0)
            return acc + _fold_rows(ones, jnp.add)
        return jnp.sum(over_chunks(add, jnp.zeros((8, tq), F32)), axis=0, keepdims=True)

    iq = iq_ref[...]
    iq_st = jnp.concatenate([iq[:, :LANES], iq[:, LANES:]], axis=0)

    def build_keys(j, carry):
        k0 = pl.multiple_of(j * kc, kc)
        la = jnp.maximum(_nt_dot(ika_ref[pl.ds(k0, kc), :], iq_st), 0.0)
        lb = jnp.maximum(_nt_dot(ikb_ref[pl.ds(k0, kc), :], iq_st), 0.0)
        score = (la[:, :tq] * iwt_ref[0:1, :] + lb[:, :tq] * iwt_ref[1:2, :]
                 + la[:, tq:] * iwt_ref[2:3, :] + lb[:, tq:] * iwt_ref[3:4, :])
        bits = pltpu.bitcast(score, I32)
        keys = bits ^ ((bits >> 31) & jnp.int32(0x7FFFFFFF))
        keys = jnp.where(bits == int_min, 0, keys)
        key_scr[j] = jnp.where(k0 + krow <= qpos, keys, int_min)
        return carry

    over_chunks(build_keys, 0)

    def value_bit(i, tu):
        cand = tu | lax.shift_left(jnp.int32(1), 31 - i)
        cnt = count_keys(lambda j, keys: keys >= (cand ^ int_min))
        return jnp.where(cnt >= top_k, cand, tu)

    thr = lax.fori_loop(0, 32, value_bit, jnp.zeros((1, tq), I32)) ^ int_min
    need = top_k - count_keys(lambda j, keys: keys > thr)

    excess = jnp.where(thr == int_min, 0.0, count_keys(lambda j, keys: keys == thr) - need)
    index_bits = (kv_ref.shape[0] - 1).bit_length()
    cut_scr[...] = jnp.full(cut_scr.shape, 1 << index_bits, I32)

    @pl.when(jnp.max(excess) > 0.0)
    def _():
        def index_bit(i, x):
            cand = x | lax.shift_left(jnp.int32(1), index_bits - 1 - i)
            cnt = count_keys(lambda j, keys: (keys == thr) & (j * kc + krow < cand))
            return jnp.where(cnt < need, cand, x)

        cut = lax.fori_loop(0, index_bits, index_bit, jnp.zeros((1, tq), I32))
        cut_scr[...] = jnp.broadcast_to(cut, cut_scr.shape)

    cut = cut_scr[0:1, :]

    def build_mask(j, carry):
        keys = key_scr[j]
        kpos = j * kc + krow
        chosen = ((keys > thr) | ((keys == thr) & (kpos <= cut))) & (kpos <= qpos)
        sel_scr[j] = jnp.where(chosen, 0.0, -jnp.inf)
        return carry

    over_chunks(build_mask, 0)

    for p in range(DSA_HEADS // 2):
        t = jnp.dot(cq_ref[p], mq_ref[p], preferred_element_type=F32)
        qcat_scr[(2 * p) * tq:(2 * p + 1) * tq, :] = t[:, :2 * LANES].astype(BF16)
        qcat_scr[(2 * p + 1) * tq:(2 * p + 2) * tq, :] = t[:, 2 * LANES:].astype(BF16)

    rows = DSA_HEADS * tq

    m_scr[...] = jnp.full(m_scr.shape, -jnp.inf, F32)
    l_scr[...] = jnp.zeros(l_scr.shape, F32)
    acc_scr[...] = jnp.zeros(acc_scr.shape, F32)

    def attend(j, carry):
        k0 = pl.multiple_of(j * kc, kc)
        s = _nt_dot(kv_ref[pl.ds(k0, kc), :], qcat_scr[...])
        s = s + jnp.concatenate([sel_scr[j]] * DSA_HEADS, axis=1)
        m = m_scr[...]
        m_new = jnp.maximum(m, jnp.max(_fold_rows(s, jnp.maximum), axis=0, keepdims=True))
        m_safe = jnp.where(m_new == -jnp.inf, 0.0, m_new)
        p = jnp.exp(s - m_safe)
        alpha = jnp.exp(m - m_safe)
        m_scr[...] = m_new
        l_scr[...] = alpha * l_scr[...] + jnp.sum(_fold_rows(p, jnp.add), axis=0, keepdims=True)
        acc_scr[...] = alpha * acc_scr[...] + jnp.dot(ckvt_ref[j], p.astype(BF16), preferred_element_type=F32)
        return carry

    over_chunks(attend, 0)
    o_lat = (acc_scr[...] / l_scr[...]).astype(BF16)
    for p in range(DSA_HEADS // 2):
        pair = jnp.concatenate([o_lat[:, (2 * p) * tq:(2 * p + 1) * tq],
                                o_lat[:, (2 * p + 1) * tq:(2 * p + 2) * tq]], axis=0)
        out_t = jnp.dot(wuvt_ref[p], pair, preferred_element_type=F32)
        o_ref[p] = out_t.T.astype(o_ref.dtype)


def _dsa_attention(iq, iwt, cq, ika, ikb, kv, ckvt, mq, wuvt, top_k, batch, seq):
    tq, kc = DSA_TQ, DSA_KC
    nq = seq // tq
    qmap = lambda b, i: (b * nq + i, 0)
    pmap = lambda b, i: (0, b * nq + i, 0)
    kmap = lambda b, i: (b, 0)
    cmap = lambda b, i: (0, 0, 0)
    pairs = cq.shape[0]
    return pl.pallas_call(
        functools.partial(_dsa_body, top_k),
        grid=(batch, nq),
        in_specs=[pl.BlockSpec((tq, iq.shape[1]), qmap),
                  pl.BlockSpec((IDX_HEADS, tq), lambda b, i: (0, b * nq + i)),
                  pl.BlockSpec((pairs, tq, LANES), pmap),
                  pl.BlockSpec((seq, LANES), kmap),
                  pl.BlockSpec((seq, LANES), kmap),
                  pl.BlockSpec((seq, 2 * LANES), kmap),
                  pl.BlockSpec((None, seq // kc, DSA_KV_RANK, kc), lambda b, i: (b, 0, 0, 0)),
                  pl.BlockSpec(mq.shape, cmap),
                  pl.BlockSpec(wuvt.shape, cmap)],
        out_specs=pl.BlockSpec((pairs, tq, LANES), pmap),
        out_shape=jax.ShapeDtypeStruct(cq.shape, BF16),
        scratch_shapes=[pltpu.VMEM((seq // kc, kc, tq), I32), pltpu.VMEM((seq // kc, kc, tq), F32),
                        pltpu.VMEM((DSA_HEADS * tq, 2 * LANES), BF16), pltpu.VMEM((8, tq), I32),
                        pltpu.VMEM((1, DSA_HEADS * tq), F32), pltpu.VMEM((1, DSA_HEADS * tq), F32),
                        pltpu.VMEM((DSA_KV_RANK, DSA_HEADS * tq), F32)],
        compiler_params=_cparams(("parallel", "arbitrary")),
        name="dsa_attention",
    )(iq, iwt, cq, ika, ikb, kv, ckvt, mq, wuvt)


def _rope_tables(positions):
    inv_freq = ROPE_THETA ** (-jnp.arange(0, ROT_DIM, 2, dtype=F32) / ROT_DIM)
    ang = positions.astype(F32).reshape(-1, 1) * inv_freq
    cos, sin = jnp.cos(ang), jnp.sin(ang)
    t = ang.shape[0]
    pad = jnp.zeros((t, HEAD_DIM - ROT_DIM), F32)
    zero8 = jnp.zeros((t, ROT_HALF), F32)
    c64 = jnp.concatenate([cos, cos, pad + 1.0], axis=1)
    sa64 = jnp.concatenate([-sin, zero8, pad], axis=1)
    sb64 = jnp.concatenate([zero8, sin, pad], axis=1)
    return tuple(jnp.tile(a, (1, LANES // HEAD_DIM)) for a in (c64, sa64, sb64))


def _pad_cols(w, width):
    return jnp.pad(w, ((0, 0), (0, width - w.shape[1])))


EVEN_GROUPS = ((FOX_WIDTH, "plain", False), (FOX_WIDTH, "plain", False), (FOX_WIDTH, "plain", False),
               (DIFF_WIDTH, "rope", False), (DIFF_WIDTH, "rope", False), (DIFF_WIDTH, "plain", False),
               (LANES, "f32", False))
ODD_GROUPS = ((DSA_WIDTH, "rope", True), (IDX_HEADS * IDX_DIM, "rope", False), (LANES, "rope", False),
              (LANES, "rope", False), (2 * LANES, "kv", False),
              (SB_WIDTH, "plain", False), (SB_WIDTH, "plain", False), (SB_WIDTH, "plain", False),
              (LANES, "f32", False))


def _pack_even(w):
    aq, ak, av, af, bq, bk, bv = jnp.split(w, (512, 1024, 1536, 1544, 2056, 2568), axis=1)
    return jnp.concatenate([aq * QK_SCALE, ak, av, bq * QK_SCALE, bk, bv, _pad_cols(af, LANES)],
                           axis=1).astype(BF16)


def _pack_odd(w):
    cq, ckv, ckr, ciq, ciw, cik, sq, sk, sv = jnp.split(w, (512, 640, 656, 912, 916, 980, 1492, 2004), axis=1)
    zeros64 = jnp.zeros_like(cik)
    return jnp.concatenate([cq * QK_SCALE, ciq,
                            jnp.concatenate([cik, zeros64], axis=1), jnp.concatenate([zeros64, cik], axis=1),
                            ckv, _pad_cols(ckr, LANES),
                            sq * QK_SCALE, sk, sv, _pad_cols(ciw * (IDX_HEADS ** -0.5), LANES)],
                           axis=1).astype(BF16)


def _pack_dsa_weights(w_uk, w_uv):
    mq, wuv = [], []
    eye = jnp.eye(ROT_DIM, dtype=F32)
    for p in range(DSA_HEADS // 2):
        m = jnp.zeros((LANES, 4 * LANES), F32)
        u = jnp.zeros((2 * DSA_KV_RANK, LANES), F32)
        for hh in range(2):
            h = 2 * p + hh
            r0 = hh * HEAD_DIM
            c0 = hh * 2 * LANES
            m = m.at[r0 + ROT_DIM:r0 + HEAD_DIM, c0:c0 + DSA_KV_RANK].set(w_uk[:, h, :].T)
            m = m.at[r0:r0 + ROT_DIM, c0 + DSA_KV_RANK:c0 + DSA_KV_RANK + ROT_DIM].set(eye)
            u = u.at[hh * DSA_KV_RANK:(hh + 1) * DSA_KV_RANK, r0:r0 + HEAD_DIM].set(w_uv[:, h, :])
        mq.append(m)
        wuv.append(u)
    return jnp.stack(mq).astype(BF16), jnp.stack(wuv).astype(BF16)


def kernel(x, positions, attn_norm, ffn_norm, final_norm, ev_w_in, ev_fgate_b, ev_lambda_q1, ev_lambda_k1,
           ev_lambda_q2, ev_lambda_k2, ev_subln, ev_w_out, od_w_in, od_kv_norm, od_w_uk, od_w_uv, od_w_out,
           ffn_w_in, ffn_conv_w, ffn_conv_b, ffn_w_out):
    batch, seq, _ = x.shape
    depth = attn_norm.shape[0]
    t = batch * seq
    top_k = min(IDX_TOPK_MAX, seq // 4)
    tabs = _rope_tables(positions)
    xf = x.reshape(t, D_MODEL)
    ones_kv = jnp.ones((LANES,), F32)
    for layer in range(depth):
        j = layer // 2
        if layer % 2 == 0:
            aq, ak, av, bq, bk, bv, af = _in_proj(xf, attn_norm[layer], _pack_even(ev_w_in[j]), tabs, ones_kv,
                                                  EVEN_GROUPS)
            c = _fgate_cumsum(af, _pad_cols(ev_fgate_b[j].reshape(1, -1), LANES), batch, seq)
            c8 = c.reshape(batch, seq, LANES)[:, :, :FOX_WIDTH // HEAD_DIM]
            ccol = c8.reshape(batch, seq, -1, 2).transpose(0, 2, 1, 3)
            crow = c8.reshape(batch, seq, -1, 2).transpose(0, 2, 3, 1)
            o_a = _fox_attention(aq, ak, av, ccol, crow, batch, seq)
            lam_init = 0.8 - 0.6 * math.exp(-0.3 * layer)
            o_b = _diff_attention(bq, bk, bv, ev_lambda_q1[j], ev_lambda_k1[j], ev_lambda_q2[j], ev_lambda_k2[j],
                                  ev_subln[j], lam_init, batch, seq)
            xf = _out_proj(xf, o_a, o_b, ev_w_out[j].astype(BF16))
        else:
            cq, iq, ika, ikb, kv, sq, sk, sv, iw = _in_proj(xf, attn_norm[layer], _pack_odd(od_w_in[j]), tabs,
                                                            od_kv_norm[j], ODD_GROUPS)
            mq, wuv = _pack_dsa_weights(od_w_uk[j], od_w_uv[j])
            ckvt = kv[:, :DSA_KV_RANK].reshape(batch, seq // DSA_KC, DSA_KC, DSA_KV_RANK).transpose(0, 1, 3, 2)
            o_c = _dsa_attention(iq, iw[:, :IDX_HEADS].T, cq, ika, ikb, kv, ckvt, mq, wuv.transpose(0, 2, 1),
                                 top_k, batch, seq)
            o_d = _sb_attention(sq, sk, sv, batch, seq)
            xf = _out_proj(xf, o_c, o_d, od_w_out[j].astype(BF16))
        xf = _ffn(xf, ffn_norm[layer], ffn_w_in[layer].astype(BF16), ffn_conv_w[layer], ffn_conv_b[layer],
                  ffn_w_out[layer].astype(BF16), final_norm, seq, layer == depth - 1)
    return xf.reshape(batch, seq, D_MODEL)
```

```python
import functools
import math

import jax
import jax.numpy as jnp
from jax import lax
from jax.experimental import pallas as pl
from jax.experimental.pallas import tpu as pltpu

F32 = jnp.float32
BF16 = jnp.bfloat16
I32 = jnp.int32

D_MODEL = 1024
HEAD_DIM = 64
ROT_DIM = 16
ROT_HALF = 8
ROPE_THETA = 500000.0
EPS = 1e-6
LANES = 128
QK_SCALE = HEAD_DIM ** -0.5
FOX_WIDTH = 512
DIFF_WIDTH = 512
DIFF_HEADS = 4
DSA_WIDTH = 512
DSA_HEADS = 8
DSA_NOPE = HEAD_DIM - ROT_DIM
DSA_KV_RANK = 128
IDX_HEADS = 4
IDX_DIM = 64
IDX_TOPK_MAX = 256
SB_WIDTH = 512
D_FF = 2816
INT_MIN = -(2 ** 31)

VMEM_LIMIT = 56 * 1024 * 1024

PROJ_TM = 512
PROJ_CHUNK = 256
FFN_TM = 512
FFN_HALO = 16
FFN_CHUNK = 256
ATT_TQ = 256
ATT_CHUNK = 1024
DSA_TQ = 256
DSA_KC = 256


def _cparams(sem):
    return pltpu.CompilerParams(dimension_semantics=sem, vmem_limit_bytes=VMEM_LIMIT)


def _rms_rows(x, g):
    ms = jnp.mean(x * x, axis=-1, keepdims=True)
    return x * lax.rsqrt(ms + EPS) * g


def _rope128(z, c, sa, sb):
    return z * c + pltpu.roll(z, LANES - ROT_HALF, axis=1) * sa + pltpu.roll(z, ROT_HALF, axis=1) * sb


def _in_proj_body(groups, x_ref, g_ref, w_ref, c_ref, sa_ref, sb_ref, kvg_ref, *rest):
    out_refs, h_scr = rest[:-1], rest[-1]
    h_scr[...] = _rms_rows(x_ref[...], g_ref[...]).astype(BF16)
    col = 0
    for (width, kind, block_major), o_ref in zip(groups, out_refs):
        cw = min(width, PROJ_CHUNK)
        for c0 in range(0, width, cw):
            zc = jnp.dot(h_scr[...], w_ref[:, col + c0:col + c0 + cw], preferred_element_type=F32)
            for l0 in range(0, cw, LANES):
                z = zc[:, l0:l0 + LANES]
                if kind == "rope" or (kind == "kv" and c0 + l0 == DSA_KV_RANK):
                    z = _rope128(z, c_ref[...], sa_ref[...], sb_ref[...])
                elif kind == "kv":
                    z = _rms_rows(z, kvg_ref[...])
                if block_major:
                    o_ref[(c0 + l0) // LANES] = z.astype(o_ref.dtype)
                else:
                    o_ref[:, c0 + l0:c0 + l0 + LANES] = z.astype(o_ref.dtype)
        col += width


def _in_proj(x, gain, w_cat, tabs, kv_gain, groups):
    t = x.shape[0]
    n = w_cat.shape[1]
    tm = PROJ_TM
    row = lambda i: (i, 0)
    const = lambda i: (0, 0)
    out_shape, out_specs = [], []
    for w, kind, block_major in groups:
        dtype = F32 if kind == "f32" else BF16
        if block_major:
            out_shape.append(jax.ShapeDtypeStruct((w // LANES, t, LANES), dtype))
            out_specs.append(pl.BlockSpec((w // LANES, tm, LANES), lambda i: (0, i, 0)))
        else:
            out_shape.append(jax.ShapeDtypeStruct((t, w), dtype))
            out_specs.append(pl.BlockSpec((tm, w), row))
    return pl.pallas_call(
        functools.partial(_in_proj_body, groups),
        grid=(t // tm,),
        in_specs=[pl.BlockSpec((tm, D_MODEL), row),
                  pl.BlockSpec((1, D_MODEL), const),
                  pl.BlockSpec((D_MODEL, n), const),
                  pl.BlockSpec((tm, LANES), row),
                  pl.BlockSpec((tm, LANES), row),
                  pl.BlockSpec((tm, LANES), row),
                  pl.BlockSpec((1, LANES), const)],
        out_specs=out_specs,
        out_shape=out_shape,
        scratch_shapes=[pltpu.VMEM((tm, D_MODEL), BF16)],
        compiler_params=_cparams(("parallel",)),
        name="in_proj",
    )(x, gain.reshape(1, D_MODEL), w_cat, tabs[0], tabs[1], tabs[2], kv_gain.reshape(1, LANES))


def _out_proj_body(x_ref, a_ref, b_ref, w_ref, o_ref):
    if len(a_ref.shape) == 3:
        a = jnp.concatenate([a_ref[i] for i in range(a_ref.shape[0])], axis=1)
    else:
        a = a_ref[...]
    half = a.shape[1]
    y = jnp.dot(a, w_ref[:half, :], preferred_element_type=F32)
    y = y + jnp.dot(b_ref[...], w_ref[half:, :], preferred_element_type=F32)
    o_ref[...] = x_ref[...] + y


def _out_proj(x, a, b, w):
    t = x.shape[0]
    tm = PROJ_TM
    row = lambda i: (i, 0)
    const = lambda i: (0, 0)
    a_spec = (pl.BlockSpec((a.shape[0], tm, LANES), lambda i: (0, i, 0)) if a.ndim == 3
              else pl.BlockSpec((tm, a.shape[1]), row))
    return pl.pallas_call(
        _out_proj_body,
        grid=(t // tm,),
        in_specs=[pl.BlockSpec((tm, D_MODEL), row),
                  a_spec,
                  pl.BlockSpec((tm, b.shape[1]), row),
                  pl.BlockSpec((D_MODEL, D_MODEL), const)],
        out_specs=pl.BlockSpec((tm, D_MODEL), row),
        out_shape=jax.ShapeDtypeStruct((t, D_MODEL), F32),
        compiler_params=_cparams(("parallel",)),
        name="out_proj",
    )(x, a, b, w)


def _ffn_body(tiles_per_seq, final, x_ref, xh_ref, g_ref, win_ref, cw_ref, cb_ref, wout_ref, fg_ref,
              o_ref, h_scr, act_scr):
    tm = x_ref.shape[0]
    x = x_ref[...]
    g = g_ref[...]
    keep = (pl.program_id(0) % tiles_per_seq != 0).astype(F32)
    h_scr[:FFN_HALO, :] = (_rms_rows(xh_ref[...], g) * keep).astype(BF16)
    h_scr[FFN_HALO:, :] = _rms_rows(x, g).astype(BF16)

    def conv(u, c0):
        y = (pltpu.roll(u, 2, axis=0) * cw_ref[0:1, c0:c0 + FFN_CHUNK]
             + pltpu.roll(u, 1, axis=0) * cw_ref[1:2, c0:c0 + FFN_CHUNK]
             + u * cw_ref[2:3, c0:c0 + FFN_CHUNK])
        return y[FFN_HALO:, :] + cb_ref[0:1, c0:c0 + FFN_CHUNK]

    for ci in range(D_FF // FFN_CHUNK):
        cg = ci * FFN_CHUNK
        cv = D_FF + cg
        ug = jnp.dot(h_scr[...], win_ref[:, cg:cg + FFN_CHUNK], preferred_element_type=F32)
        uv = jnp.dot(h_scr[...], win_ref[:, cv:cv + FFN_CHUNK], preferred_element_type=F32)
        gate = conv(ug, cg)
        val = conv(uv, cv)
        act_scr[:, cg:cg + FFN_CHUNK] = (gate * jax.nn.sigmoid(gate) * val).astype(BF16)
    y = x + jnp.dot(act_scr[...], wout_ref[...], preferred_element_type=F32)
    if final:
        y = _rms_rows(y, fg_ref[...])
    o_ref[...] = y


def _ffn(x, gain, w_in, conv_w, conv_b, w_out, final_gain, seq, final):
    t = x.shape[0]
    tm = FFN_TM
    halo_blocks = tm // FFN_HALO
    row = lambda i: (i, 0)
    const = lambda i: (0, 0)
    return pl.pallas_call(
        functools.partial(_ffn_body, seq // tm, final),
        grid=(t // tm,),
        in_specs=[pl.BlockSpec((tm, D_MODEL), row),
                  pl.BlockSpec((FFN_HALO, D_MODEL), lambda i: (jnp.maximum(i * halo_blocks - 1, 0), 0)),
                  pl.BlockSpec((1, D_MODEL), const),
                  pl.BlockSpec((D_MODEL, 2 * D_FF), const),
                  pl.BlockSpec((3, 2 * D_FF), const),
                  pl.BlockSpec((1, 2 * D_FF), const),
                  pl.BlockSpec((D_FF, D_MODEL), const),
                  pl.BlockSpec((1, D_MODEL), const)],
        out_specs=pl.BlockSpec((tm, D_MODEL), row),
        out_shape=jax.ShapeDtypeStruct((t, D_MODEL), F32),
        scratch_shapes=[pltpu.VMEM((tm + FFN_HALO, D_MODEL), BF16), pltpu.VMEM((tm, D_FF), BF16)],
        compiler_params=_cparams(("parallel",)),
        name="conv_ffn",
    )(x, x, gain.reshape(1, D_MODEL), w_in, conv_w, conv_b.reshape(1, 2 * D_FF), w_out,
      final_gain.reshape(1, D_MODEL))


def _split3(x):
    hi = x.astype(BF16)
    r = x - hi.astype(F32)
    mid = r.astype(BF16)
    lo = (r - mid.astype(F32)).astype(BF16)
    return hi, mid, lo


def _fgate_body(af_ref, b_ref, c_ref):
    s = af_ref.shape[0]
    blk = 512
    ri = lax.broadcasted_iota(I32, (blk, blk), 0)
    ci = lax.broadcasted_iota(I32, (blk, blk), 1)
    tri = jnp.where(ci <= ri, 1.0, 0.0).astype(BF16)
    carry = jnp.zeros((1, LANES), F32)
    for r0 in range(0, s, blk):
        lf = jax.nn.log_sigmoid(af_ref[r0:r0 + blk, :] + b_ref[...])
        hi, mid, lo = _split3(lf)
        c = (jnp.dot(tri, hi, preferred_element_type=F32)
             + jnp.dot(tri, mid, preferred_element_type=F32)
             + jnp.dot(tri, lo, preferred_element_type=F32)) + carry
        c_ref[r0:r0 + blk, :] = c
        carry = c[blk - 1:blk, :]


def _fgate_cumsum(af, bias, batch, seq):
    return pl.pallas_call(
        _fgate_body,
        grid=(batch,),
        in_specs=[pl.BlockSpec((seq, LANES), lambda b: (b, 0)),
                  pl.BlockSpec((1, LANES), lambda b: (0, 0))],
        out_specs=pl.BlockSpec((seq, LANES), lambda b: (b, 0)),
        out_shape=jax.ShapeDtypeStruct((batch * seq, LANES), F32),
        compiler_params=_cparams(("parallel",)),
        name="fox_gate_cumsum",
    )(af, bias)


def _nt_dot(a, b):
    return lax.dot_general(a, b, (((1,), (1,)), ((), ())), preferred_element_type=F32)


def _per_query_block(qi, n, fn):
    for c in range(n):
        @pl.when(qi == c)
        def _():
            fn(c)


def _stack_heads(q):
    lane = lax.broadcasted_iota(I32, (1, LANES), 1)
    zero = jnp.zeros_like(q)
    return jnp.concatenate([jnp.where(lane < HEAD_DIM, q, zero), jnp.where(lane >= HEAD_DIM, q, zero)], axis=0)


def _unstack_heads(o):
    tq = o.shape[0] // 2
    lane = lax.broadcasted_iota(I32, (1, LANES), 1)
    return jnp.where(lane < HEAD_DIM, o[:tq], o[tq:])


def _diag_mask(rows, tq, strict):
    qpos = lax.broadcasted_iota(I32, (rows, tq), 0) & (tq - 1)
    kpos = lax.broadcasted_iota(I32, (rows, tq), 1)
    return kpos < qpos if strict else kpos <= qpos


def _fill_diag(s, tq, strict, fill):
    nk = s.shape[1]
    d = jnp.where(_diag_mask(s.shape[0], tq, strict), s[:, nk - tq:], fill)
    return d if nk == tq else jnp.concatenate([s[:, :nk - tq], d], axis=1)


def _fold_rows(x, op):
    while x.shape[0] > 8:
        half = x.shape[0] // 2
        x = op(x[:half], x[half:])
    return x


def _online_softmax_pv(rows, nk, score_chunk, v_ref):
    m = jnp.full((rows, 1), -jnp.inf, F32)
    l = jnp.zeros((rows, 1), F32)
    acc = jnp.zeros((rows, LANES), F32)
    for k0 in range(0, nk, ATT_CHUNK):
        k1 = min(k0 + ATT_CHUNK, nk)
        s = score_chunk(k0, k1)
        m_new = jnp.maximum(m, jnp.max(s, axis=-1, keepdims=True))
        p = jnp.exp(s - m_new)
        alpha = jnp.exp(m - m_new)
        l = alpha * l + jnp.sum(p, axis=-1, keepdims=True)
        acc = alpha * acc + jnp.dot(p.astype(BF16), v_ref[k0:k1, :], preferred_element_type=F32)
        m = m_new
    return acc / l


def _fox_block(c, q_ref, k_ref, v_ref, ccol_ref, crow_ref, o_ref):
    tq = ATT_TQ
    nk = (c + 1) * tq
    q_st = _stack_heads(q_ref[...])

    def score_chunk(k0, k1):
        s = _nt_dot(q_st, k_ref[k0:k1, :])
        bias = jnp.concatenate([ccol_ref[:, 0:1] - crow_ref[0:1, k0:k1],
                                ccol_ref[:, 1:2] - crow_ref[1:2, k0:k1]], axis=0)
        return _fill_diag(s + bias, tq, False, -jnp.inf) if k1 == nk else s + bias

    o_ref[...] = _unstack_heads(_online_softmax_pv(2 * tq, nk, score_chunk, v_ref)).astype(o_ref.dtype)


def _fox_body(*refs):
    _per_query_block(pl.program_id(2), refs[1].shape[0] // ATT_TQ, lambda c: _fox_block(c, *refs))


def _fox_attention(q, k, v, ccol, crow, batch, seq):
    tq = ATT_TQ
    nq = seq // tq
    pairs = q.shape[1] // LANES
    qmap = lambda b, p, i: (b * nq + i, p)
    kvmap = lambda b, p, i: (b, p)
    return pl.pallas_call(
        _fox_body,
        grid=(batch, pairs, nq),
        in_specs=[pl.BlockSpec((tq, LANES), qmap),
                  pl.BlockSpec((seq, LANES), kvmap),
                  pl.BlockSpec((seq, LANES), kvmap),
                  pl.BlockSpec((None, None, tq, 2), lambda b, p, i: (b, p, i, 0)),
                  pl.BlockSpec((None, None, 2, seq), lambda b, p, i: (b, p, 0, 0))],
        out_specs=pl.BlockSpec((tq, LANES), qmap),
        out_shape=jax.ShapeDtypeStruct(q.shape, BF16),
        compiler_params=_cparams(("parallel", "parallel", "arbitrary")),
        name="fox_attention",
    )(q, k, v, ccol, crow)


def _diff_block(c, lam_init, q_ref, k_ref, v_ref, lq1_ref, lk1_ref, lq2_ref, lk2_ref, sub_ref, o_ref):
    tq = ATT_TQ
    nk = (c + 1) * tq
    q_st = _stack_heads(q_ref[...])

    def score_chunk(k0, k1):
        s = _nt_dot(q_st, k_ref[k0:k1, :])
        return _fill_diag(s, tq, False, -jnp.inf) if k1 == nk else s

    o = _online_softmax_pv(2 * tq, nk, score_chunk, v_ref)
    lam = (jnp.exp(jnp.sum(lq1_ref[...] * lk1_ref[...], axis=-1, keepdims=True))
           - jnp.exp(jnp.sum(lq2_ref[...] * lk2_ref[...], axis=-1, keepdims=True)) + lam_init)
    o = o[:tq] - lam * o[tq:]
    o_ref[...] = (_rms_rows(o, sub_ref[...]) * (1.0 - lam_init)).astype(o_ref.dtype)


def _diff_body(lam_init, *refs):
    _per_query_block(pl.program_id(2), refs[1].shape[0] // ATT_TQ, lambda c: _diff_block(c, lam_init, *refs))


def _diff_attention(q, k, v, lq1, lk1, lq2, lk2, subln, lam_init, batch, seq):
    tq = ATT_TQ
    nq = seq // tq
    heads = q.shape[1] // LANES
    qmap = lambda b, h, i: (b * nq + i, h)
    kvmap = lambda b, h, i: (b, h)
    vec = lambda n: pl.BlockSpec((1, n), lambda b, h, i: (0, 0))
    return pl.pallas_call(
        functools.partial(_diff_body, lam_init),
        grid=(batch, heads, nq),
        in_specs=[pl.BlockSpec((tq, LANES), qmap),
                  pl.BlockSpec((seq, LANES), kvmap),
                  pl.BlockSpec((seq, LANES), kvmap),
                  vec(HEAD_DIM), vec(HEAD_DIM), vec(HEAD_DIM), vec(HEAD_DIM), vec(LANES)],
        out_specs=pl.BlockSpec((tq, LANES), qmap),
        out_shape=jax.ShapeDtypeStruct(q.shape, BF16),
        compiler_params=_cparams(("parallel", "parallel", "arbitrary")),
        name="diff_attention",
    )(q, k, v, lq1.reshape(1, -1), lk1.reshape(1, -1), lq2.reshape(1, -1), lk2.reshape(1, -1),
      subln.reshape(1, -1))


def _sb_block(c, q_ref, k_ref, v_ref, o_ref):
    tq = tk = ATT_TQ
    nk = (c + 1) * tq
    z = _nt_dot(_stack_heads(q_ref[...]), k_ref[0:nk, :])
    log1m = -jnp.maximum(z, 0.0) - jnp.log(1.0 + jnp.exp(-jnp.abs(z)))
    log1m = _fill_diag(log1m, tq, True, 0.0)
    ri = lax.broadcasted_iota(I32, (2 * tk, tk), 0) & (tk - 1)
    ci = lax.broadcasted_iota(I32, (2 * tk, tk), 1)
    later = jnp.where(ri > ci, 1.0, 0.0).astype(BF16)
    run = jnp.zeros((2 * tq, 1), F32)
    acc = jnp.zeros((2 * tq, LANES), F32)
    for blk in reversed(range(nk // tk)):
        lb = log1m[:, blk * tk:(blk + 1) * tk]
        hi = lb.astype(BF16)
        lo = (lb - hi.astype(F32)).astype(BF16)
        local = jnp.dot(jnp.concatenate([hi, lo], axis=1), later, preferred_element_type=F32)
        a = jnp.exp(z[:, blk * tk:(blk + 1) * tk] + lb + (local + run))
        if blk == c:
            a = jnp.where(_diag_mask(2 * tq, tq, True), a, 0.0)
        acc = acc + jnp.dot(a.astype(BF16), v_ref[blk * tk:(blk + 1) * tk, :], preferred_element_type=F32)
        run = run + (local[:, 0:1] + lb[:, 0:1])
    o_ref[...] = _unstack_heads(acc).astype(o_ref.dtype)


def _sb_body(*refs):
    _per_query_block(pl.program_id(2), refs[1].shape[0] // ATT_TQ, lambda c: _sb_block(c, *refs))


def _sb_attention(q, k, v, batch, seq):
    tq = ATT_TQ
    nq = seq // tq
    pairs = q.shape[1] // LANES
    qmap = lambda b, p, i: (b * nq + i, p)
    kvmap = lambda b, p, i: (b, p)
    return pl.pallas_call(
        _sb_body,
        grid=(batch, pairs, nq),
        in_specs=[pl.BlockSpec((tq, LANES), qmap),
                  pl.BlockSpec((seq, LANES), kvmap),
                  pl.BlockSpec((seq, LANES), kvmap)],
        out_specs=pl.BlockSpec((tq, LANES), qmap),
        out_shape=jax.ShapeDtypeStruct(q.shape, BF16),
        compiler_params=_cparams(("parallel", "parallel", "arbitrary")),
        name="stickbreak_attention",
    )(q, k, v)


def _dsa_body(top_k, iq_ref, iwt_ref, cq_ref, ika_ref, ikb_ref, kv_ref, ckvt_ref, mq_ref, wuvt_ref,
              o_ref, key_scr, sel_scr, qcat_scr, cut_scr, m_scr, l_scr, acc_scr, s_scr):
    tq, kc = DSA_TQ, DSA_KC
    qi = pl.program_id(1)
    n_chunks = qi + 1
    int_min = jnp.int32(INT_MIN)
    qpos = qi * tq + lax.broadcasted_iota(I32, (kc, tq), 1)
    krow = lax.broadcasted_iota(I32, (kc, tq), 0)

    def over_chunks(fn, init):
        return lax.fori_loop(0, n_chunks, fn, init)

    def count_keys(pred_of_chunk):
        def add(j, acc):
            ones = jnp.where(pred_of_chunk(j, key_scr[j]), 1.0, 0.0)
            return acc + _fold_rows(ones, jnp.add)
        return jnp.sum(over_chunks(add, jnp.zeros((8, tq), F32)), axis=0, keepdims=True)

    iq = iq_ref[...]
    iq_st = jnp.concatenate([iq[:, :LANES], iq[:, LANES:]], axis=0)

    def build_keys(j, carry):
        k0 = pl.multiple_of(j * kc, kc)
        la = jnp.maximum(_nt_dot(ika_ref[pl.ds(k0, kc), :], iq_st), 0.0)
        lb = jnp.maximum(_nt_dot(ikb_ref[pl.ds(k0, kc), :], iq_st), 0.0)
        score = (la[:, :tq] * iwt_ref[0:1, :] + lb[:, :tq] * iwt_ref[1:2, :]
                 + la[:, tq:] * iwt_ref[2:3, :] + lb[:, tq:] * iwt_ref[3:4, :])
        bits = pltpu.bitcast(score, I32)
        keys = bits ^ ((bits >> 31) & jnp.int32(0x7FFFFFFF))
        keys = jnp.where(bits == int_min, 0, keys)
        key_scr[j] = jnp.where(k0 + krow <= qpos, keys, int_min)
        return carry

    over_chunks(build_keys, 0)

    def value_bit(i, tu):
        cand = tu | lax.shift_left(jnp.int32(1), 31 - i)
        cnt = count_keys(lambda j, keys: keys >= (cand ^ int_min))
        return jnp.where(cnt >= top_k, cand, tu)

    thr = lax.fori_loop(0, 32, value_bit, jnp.zeros((1, tq), I32)) ^ int_min
    need = top_k - count_keys(lambda j, keys: keys > thr)

    excess = jnp.where(thr == int_min, 0.0, count_keys(lambda j, keys: keys == thr) - need)
    index_bits = (kv_ref.shape[0] - 1).bit_length()
    cut_scr[...] = jnp.full(cut_scr.shape, 1 << index_bits, I32)

    @pl.when(jnp.max(excess) > 0.0)
    def _():
        def index_bit(i, x):
            cand = x | lax.shift_left(jnp.int32(1), index_bits - 1 - i)
            cnt = count_keys(lambda j, keys: (keys == thr) & (j * kc + krow < cand))
            return jnp.where(cnt < need, cand, x)

        cut = lax.fori_loop(0, index_bits, index_bit, jnp.zeros((1, tq), I32))
        cut_scr[...] = jnp.broadcast_to(cut, cut_scr.shape)

    cut = cut_scr[0:1, :]

    def build_mask(j, carry):
        keys = key_scr[j]
        kpos = j * kc + krow
        chosen = ((keys > thr) | ((keys == thr) & (kpos <= cut))) & (kpos <= qpos)
        sel_scr[j] = jnp.where(chosen, 0.0, -jnp.inf)
        return carry

    over_chunks(build_mask, 0)

    for p in range(DSA_HEADS // 2):
        t = jnp.dot(cq_ref[p], mq_ref[p], preferred_element_type=F32)
        qcat_scr[(2 * p) * tq:(2 * p + 1) * tq, :] = t[:, :2 * LANES].astype(BF16)
        qcat_scr[(2 * p + 1) * tq:(2 * p + 2) * tq, :] = t[:, 2 * LANES:].astype(BF16)

    rows = DSA_HEADS * tq

    m_scr[...] = jnp.full(m_scr.shape, -jnp.inf, F32)
    l_scr[...] = jnp.zeros(l_scr.shape, F32)
    acc_scr[...] = jnp.zeros(acc_scr.shape, F32)

    def scores(j):
        k0 = pl.multiple_of(j * kc, kc)
        return _nt_dot(kv_ref[pl.ds(k0, kc), :], qcat_scr[...])

    s_scr[...] = scores(0)

    def attend(j, carry):
        s = s_scr[...]
        s_scr[...] = scores(jnp.minimum(j + 1, qi))
        s = s + jnp.concatenate([sel_scr[j]] * DSA_HEADS, axis=1)
        m = m_scr[...]
        m_new = jnp.maximum(m, jnp.max(_fold_rows(s, jnp.maximum), axis=0, keepdims=True))
        m_safe = jnp.where(m_new == -jnp.inf, 0.0, m_new)
        p = jnp.exp(s - m_safe)
        alpha = jnp.exp(m - m_safe)
        m_scr[...] = m_new
        l_scr[...] = alpha * l_scr[...] + jnp.sum(_fold_rows(p, jnp.add), axis=0, keepdims=True)
        acc_scr[...] = alpha * acc_scr[...] + jnp.dot(ckvt_ref[j], p.astype(BF16), preferred_element_type=F32)
        return carry

    over_chunks(attend, 0)
    o_lat = (acc_scr[...] / l_scr[...]).astype(BF16)
    for p in range(DSA_HEADS // 2):
        pair = jnp.concatenate([o_lat[:, (2 * p) * tq:(2 * p + 1) * tq],
                                o_lat[:, (2 * p + 1) * tq:(2 * p + 2) * tq]], axis=0)
        out_t = jnp.dot(wuvt_ref[p], pair, preferred_element_type=F32)
        o_ref[p] = out_t.T.astype(o_ref.dtype)


def _dsa_attention(iq, iwt, cq, ika, ikb, kv, ckvt, mq, wuvt, top_k, batch, seq):
    tq, kc = DSA_TQ, DSA_KC
    nq = seq // tq
    qmap = lambda b, i: (b * nq + i, 0)
    pmap = lambda b, i: (0, b * nq + i, 0)
    kmap = lambda b, i: (b, 0)
    cmap = lambda b, i: (0, 0, 0)
    pairs = cq.shape[0]
    return pl.pallas_call(
        functools.partial(_dsa_body, top_k),
        grid=(batch, nq),
        in_specs=[pl.BlockSpec((tq, iq.shape[1]), qmap),
                  pl.BlockSpec((IDX_HEADS, tq), lambda b, i: (0, b * nq + i)),
                  pl.BlockSpec((pairs, tq, LANES), pmap),
                  pl.BlockSpec((seq, LANES), kmap),
                  pl.BlockSpec((seq, LANES), kmap),
                  pl.BlockSpec((seq, 2 * LANES), kmap),
                  pl.BlockSpec((None, seq // kc, DSA_KV_RANK, kc), lambda b, i: (b, 0, 0, 0)),
                  pl.BlockSpec(mq.shape, cmap),
                  pl.BlockSpec(wuvt.shape, cmap)],
        out_specs=pl.BlockSpec((pairs, tq, LANES), pmap),
        out_shape=jax.ShapeDtypeStruct(cq.shape, BF16),
        scratch_shapes=[pltpu.VMEM((seq // kc, kc, tq), I32), pltpu.VMEM((seq // kc, kc, tq), F32),
                        pltpu.VMEM((DSA_HEADS * tq, 2 * LANES), BF16), pltpu.VMEM((8, tq), I32),
                        pltpu.VMEM((1, DSA_HEADS * tq), F32), pltpu.VMEM((1, DSA_HEADS * tq), F32),
                        pltpu.VMEM((DSA_KV_RANK, DSA_HEADS * tq), F32), pltpu.VMEM((kc, DSA_HEADS * tq), F32)],
        compiler_params=_cparams(("parallel", "arbitrary")),
        name="dsa_attention",
    )(iq, iwt, cq, ika, ikb, kv, ckvt, mq, wuvt)


def _rope_tables(positions):
    inv_freq = ROPE_THETA ** (-jnp.arange(0, ROT_DIM, 2, dtype=F32) / ROT_DIM)
    ang = positions.astype(F32).reshape(-1, 1) * inv_freq
    cos, sin = jnp.cos(ang), jnp.sin(ang)
    t = ang.shape[0]
    pad = jnp.zeros((t, HEAD_DIM - ROT_DIM), F32)
    zero8 = jnp.zeros((t, ROT_HALF), F32)
    c64 = jnp.concatenate([cos, cos, pad + 1.0], axis=1)
    sa64 = jnp.concatenate([-sin, zero8, pad], axis=1)
    sb64 = jnp.concatenate([zero8, sin, pad], axis=1)
    return tuple(jnp.tile(a, (1, LANES // HEAD_DIM)) for a in (c64, sa64, sb64))


def _pad_cols(w, width):
    return jnp.pad(w, ((0, 0), (0, width - w.shape[1])))


EVEN_GROUPS = ((FOX_WIDTH, "plain", False), (FOX_WIDTH, "plain", False), (FOX_WIDTH, "plain", False),
               (DIFF_WIDTH, "rope", False), (DIFF_WIDTH, "rope", False), (DIFF_WIDTH, "plain", False),
               (LANES, "f32", False))
ODD_GROUPS = ((DSA_WIDTH, "rope", True), (IDX_HEADS * IDX_DIM, "rope", False), (LANES, "rope", False),
              (LANES, "rope", False), (2 * LANES, "kv", False),
              (SB_WIDTH, "plain", False), (SB_WIDTH, "plain", False), (SB_WIDTH, "plain", False),
              (LANES, "f32", False))


def _pack_even(w):
    aq, ak, av, af, bq, bk, bv = jnp.split(w, (512, 1024, 1536, 1544, 2056, 2568), axis=1)
    return jnp.concatenate([aq * QK_SCALE, ak, av, bq * QK_SCALE, bk, bv, _pad_cols(af, LANES)],
                           axis=1).astype(BF16)


def _pack_odd(w):
    cq, ckv, ckr, ciq, ciw, cik, sq, sk, sv = jnp.split(w, (512, 640, 656, 912, 916, 980, 1492, 2004), axis=1)
    zeros64 = jnp.zeros_like(cik)
    return jnp.concatenate([cq * QK_SCALE, ciq,
                            jnp.concatenate([cik, zeros64], axis=1), jnp.concatenate([zeros64, cik], axis=1),
                            ckv, _pad_cols(ckr, LANES),
                            sq * QK_SCALE, sk, sv, _pad_cols(ciw * (IDX_HEADS ** -0.5), LANES)],
                           axis=1).astype(BF16)


def _pack_dsa_weights(w_uk, w_uv):
    mq, wuv = [], []
    eye = jnp.eye(ROT_DIM, dtype=F32)
    for p in range(DSA_HEADS // 2):
        m = jnp.zeros((LANES, 4 * LANES), F32)
        u = jnp.zeros((2 * DSA_KV_RANK, LANES), F32)
        for hh in range(2):
            h = 2 * p + hh
            r0 = hh * HEAD_DIM
            c0 = hh * 2 * LANES
            m = m.at[r0 + ROT_DIM:r0 + HEAD_DIM, c0:c0 + DSA_KV_RANK].set(w_uk[:, h, :].T)
            m = m.at[r0:r0 + ROT_DIM, c0 + DSA_KV_RANK:c0 + DSA_KV_RANK + ROT_DIM].set(eye)
            u = u.at[hh * DSA_KV_RANK:(hh + 1) * DSA_KV_RANK, r0:r0 + HEAD_DIM].set(w_uv[:, h, :])
        mq.append(m)
        wuv.append(u)
    return jnp.stack(mq).astype(BF16), jnp.stack(wuv).astype(BF16)


def kernel(x, positions, attn_norm, ffn_norm, final_norm, ev_w_in, ev_fgate_b, ev_lambda_q1, ev_lambda_k1,
           ev_lambda_q2, ev_lambda_k2, ev_subln, ev_w_out, od_w_in, od_kv_norm, od_w_uk, od_w_uv, od_w_out,
           ffn_w_in, ffn_conv_w, ffn_conv_b, ffn_w_out):
    batch, seq, _ = x.shape
    depth = attn_norm.shape[0]
    t = batch * seq
    top_k = min(IDX_TOPK_MAX, seq // 4)
    tabs = _rope_tables(positions)
    xf = x.reshape(t, D_MODEL)
    ones_kv = jnp.ones((LANES,), F32)
    for layer in range(depth):
        j = layer // 2
        if layer % 2 == 0:
            aq, ak, av, bq, bk, bv, af = _in_proj(xf, attn_norm[layer], _pack_even(ev_w_in[j]), tabs, ones_kv,
                                                  EVEN_GROUPS)
            c = _fgate_cumsum(af, _pad_cols(ev_fgate_b[j].reshape(1, -1), LANES), batch, seq)
            c8 = c.reshape(batch, seq, LANES)[:, :, :FOX_WIDTH // HEAD_DIM]
            ccol = c8.reshape(batch, seq, -1, 2).transpose(0, 2, 1, 3)
            crow = c8.reshape(batch, seq, -1, 2).transpose(0, 2, 3, 1)
            o_a = _fox_attention(aq, ak, av, ccol, crow, batch, seq)
            lam_init = 0.8 - 0.6 * math.exp(-0.3 * layer)
            o_b = _diff_attention(bq, bk, bv, ev_lambda_q1[j], ev_lambda_k1[j], ev_lambda_q2[j], ev_lambda_k2[j],
                                  ev_subln[j], lam_init, batch, seq)
            xf = _out_proj(xf, o_a, o_b, ev_w_out[j].astype(BF16))
        else:
            cq, iq, ika, ikb, kv, sq, sk, sv, iw = _in_proj(xf, attn_norm[layer], _pack_odd(od_w_in[j]), tabs,
                                                            od_kv_norm[j], ODD_GROUPS)
            mq, wuv = _pack_dsa_weights(od_w_uk[j], od_w_uv[j])
            ckvt = kv[:, :DSA_KV_RANK].reshape(batch, seq // DSA_KC, DSA_KC, DSA_KV_RANK).transpose(0, 1, 3, 2)
            o_c = _dsa_attention(iq, iw[:, :IDX_HEADS].T, cq, ika, ikb, kv, ckvt, mq, wuv.transpose(0, 2, 1),
                                 top_k, batch, seq)
            o_d = _sb_attention(sq, sk, sv, batch, seq)
            xf = _out_proj(xf, o_c, o_d, od_w_out[j].astype(BF16))
        xf = _ffn(xf, ffn_norm[layer], ffn_w_in[layer].astype(BF16), ffn_conv_w[layer], ffn_conv_b[layer],
                  ffn_w_out[layer].astype(BF16), final_norm, seq, layer == depth - 1)
    return xf.reshape(batch, seq, D_MODEL)
```

```python
import functools
import math

import jax
import jax.numpy as jnp
from jax import lax
from jax.experimental import pallas as pl
from jax.experimental.pallas import tpu as pltpu

F32 = jnp.float32
BF16 = jnp.bfloat16
I32 = jnp.int32

D_MODEL = 1024
HEAD_DIM = 64
ROT_DIM = 16
ROT_HALF = 8
ROPE_THETA = 500000.0
EPS = 1e-6
LANES = 128
QK_SCALE = HEAD_DIM ** -0.5
FOX_WIDTH = 512
DIFF_WIDTH = 512
DIFF_HEADS = 4
DSA_WIDTH = 512
DSA_HEADS = 8
DSA_NOPE = HEAD_DIM - ROT_DIM
DSA_KV_RANK = 128
IDX_HEADS = 4
IDX_DIM = 64
IDX_TOPK_MAX = 256
SB_WIDTH = 512
D_FF = 2816
INT_MIN = -(2 ** 31)

VMEM_LIMIT = 56 * 1024 * 1024

PROJ_TM = 512
PROJ_CHUNK = 256
FFN_TM = 512
FFN_HALO = 16
FFN_CHUNK = 256
ATT_TQ = 512
SB_TQ = 256
ATT_CHUNK = 1024
DSA_TQ = 256
DSA_KC = 256
DSA_COUNT_GROUP = 1


def _cparams(sem):
    return pltpu.CompilerParams(dimension_semantics=sem, vmem_limit_bytes=VMEM_LIMIT)


def _rms_rows(x, g):
    ms = jnp.mean(x * x, axis=-1, keepdims=True)
    return x * lax.rsqrt(ms + EPS) * g


def _rope128(z, c, sa, sb):
    return z * c + pltpu.roll(z, LANES - ROT_HALF, axis=1) * sa + pltpu.roll(z, ROT_HALF, axis=1) * sb


def _in_proj_body(groups, x_ref, g_ref, w_ref, c_ref, sa_ref, sb_ref, kvg_ref, *rest):
    out_refs, h_scr = rest[:-1], rest[-1]
    h_scr[...] = _rms_rows(x_ref[...], g_ref[...]).astype(BF16)
    col = 0
    for (width, kind, block_major), o_ref in zip(groups, out_refs):
        cw = min(width, PROJ_CHUNK)
        for c0 in range(0, width, cw):
            zc = jnp.dot(h_scr[...], w_ref[:, col + c0:col + c0 + cw], preferred_element_type=F32)
            for l0 in range(0, cw, LANES):
                z = zc[:, l0:l0 + LANES]
                if kind == "rope" or (kind == "kv" and c0 + l0 == DSA_KV_RANK):
                    z = _rope128(z, c_ref[...], sa_ref[...], sb_ref[...])
                elif kind == "kv":
                    z = _rms_rows(z, kvg_ref[...])
                if block_major:
                    o_ref[(c0 + l0) // LANES] = z.astype(o_ref.dtype)
                else:
                    o_ref[:, c0 + l0:c0 + l0 + LANES] = z.astype(o_ref.dtype)
        col += width


def _in_proj(x, gain, w_cat, tabs, kv_gain, groups):
    t = x.shape[0]
    n = w_cat.shape[1]
    tm = PROJ_TM
    row = lambda i: (i, 0)
    const = lambda i: (0, 0)
    out_shape, out_specs = [], []
    for w, kind, block_major in groups:
        dtype = F32 if kind == "f32" else BF16
        if block_major:
            out_shape.append(jax.ShapeDtypeStruct((w // LANES, t, LANES), dtype))
            out_specs.append(pl.BlockSpec((w // LANES, tm, LANES), lambda i: (0, i, 0)))
        else:
            out_shape.append(jax.ShapeDtypeStruct((t, w), dtype))
            out_specs.append(pl.BlockSpec((tm, w), row))
    return pl.pallas_call(
        functools.partial(_in_proj_body, groups),
        grid=(t // tm,),
        in_specs=[pl.BlockSpec((tm, D_MODEL), row),
                  pl.BlockSpec((1, D_MODEL), const),
                  pl.BlockSpec((D_MODEL, n), const),
                  pl.BlockSpec((tm, LANES), row),
                  pl.BlockSpec((tm, LANES), row),
                  pl.BlockSpec((tm, LANES), row),
                  pl.BlockSpec((1, LANES), const)],
        out_specs=out_specs,
        out_shape=out_shape,
        scratch_shapes=[pltpu.VMEM((tm, D_MODEL), BF16)],
        compiler_params=_cparams(("parallel",)),
        name="in_proj",
    )(x, gain.reshape(1, D_MODEL), w_cat, tabs[0], tabs[1], tabs[2], kv_gain.reshape(1, LANES))


def _out_proj_body(x_ref, a_ref, b_ref, w_ref, o_ref):
    if len(a_ref.shape) == 3:
        a = jnp.concatenate([a_ref[i] for i in range(a_ref.shape[0])], axis=1)
    else:
        a = a_ref[...]
    half = a.shape[1]
    y = jnp.dot(a, w_ref[:half, :], preferred_element_type=F32)
    y = y + jnp.dot(b_ref[...], w_ref[half:, :], preferred_element_type=F32)
    o_ref[...] = x_ref[...] + y


def _out_proj(x, a, b, w):
    t = x.shape[0]
    tm = PROJ_TM
    row = lambda i: (i, 0)
    const = lambda i: (0, 0)
    a_spec = (pl.BlockSpec((a.shape[0], tm, LANES), lambda i: (0, i, 0)) if a.ndim == 3
              else pl.BlockSpec((tm, a.shape[1]), row))
    return pl.pallas_call(
        _out_proj_body,
        grid=(t // tm,),
        in_specs=[pl.BlockSpec((tm, D_MODEL), row),
                  a_spec,
                  pl.BlockSpec((tm, b.shape[1]), row),
                  pl.BlockSpec((D_MODEL, D_MODEL), const)],
        out_specs=pl.BlockSpec((tm, D_MODEL), row),
        out_shape=jax.ShapeDtypeStruct((t, D_MODEL), F32),
        compiler_params=_cparams(("parallel",)),
        name="out_proj",
    )(x, a, b, w)


def _ffn_body(tiles_per_seq, final, x_ref, xh_ref, g_ref, win_ref, cw_ref, cb_ref, wout_ref, fg_ref,
              o_ref, h_scr, act_scr):
    tm = x_ref.shape[0]
    x = x_ref[...]
    g = g_ref[...]
    keep = (pl.program_id(0) % tiles_per_seq != 0).astype(F32)
    h_scr[:FFN_HALO, :] = (_rms_rows(xh_ref[...], g) * keep).astype(BF16)
    h_scr[FFN_HALO:, :] = _rms_rows(x, g).astype(BF16)

    def conv(u, c0):
        y = (pltpu.roll(u, 2, axis=0) * cw_ref[0:1, c0:c0 + FFN_CHUNK]
             + pltpu.roll(u, 1, axis=0) * cw_ref[1:2, c0:c0 + FFN_CHUNK]
             + u * cw_ref[2:3, c0:c0 + FFN_CHUNK])
        return y[FFN_HALO:, :] + cb_ref[0:1, c0:c0 + FFN_CHUNK]

    for ci in range(D_FF // FFN_CHUNK):
        cg = ci * FFN_CHUNK
        cv = D_FF + cg
        ug = jnp.dot(h_scr[...], win_ref[:, cg:cg + FFN_CHUNK], preferred_element_type=F32)
        uv = jnp.dot(h_scr[...], win_ref[:, cv:cv + FFN_CHUNK], preferred_element_type=F32)
        gate = conv(ug, cg)
        val = conv(uv, cv)
        act_scr[:, cg:cg + FFN_CHUNK] = (gate * jax.nn.sigmoid(gate) * val).astype(BF16)
    y = x + jnp.dot(act_scr[...], wout_ref[...], preferred_element_type=F32)
    if final:
        y = _rms_rows(y, fg_ref[...])
    o_ref[...] = y


def _ffn(x, gain, w_in, conv_w, conv_b, w_out, final_gain, seq, final):
    t = x.shape[0]
    tm = FFN_TM
    halo_blocks = tm // FFN_HALO
    row = lambda i: (i, 0)
    const = lambda i: (0, 0)
    return pl.pallas_call(
        functools.partial(_ffn_body, seq // tm, final),
        grid=(t // tm,),
        in_specs=[pl.BlockSpec((tm, D_MODEL), row),
                  pl.BlockSpec((FFN_HALO, D_MODEL), lambda i: (jnp.maximum(i * halo_blocks - 1, 0), 0)),
                  pl.BlockSpec((1, D_MODEL), const),
                  pl.BlockSpec((D_MODEL, 2 * D_FF), const),
                  pl.BlockSpec((3, 2 * D_FF), const),
                  pl.BlockSpec((1, 2 * D_FF), const),
                  pl.BlockSpec((D_FF, D_MODEL), const),
                  pl.BlockSpec((1, D_MODEL), const)],
        out_specs=pl.BlockSpec((tm, D_MODEL), row),
        out_shape=jax.ShapeDtypeStruct((t, D_MODEL), F32),
        scratch_shapes=[pltpu.VMEM((tm + FFN_HALO, D_MODEL), BF16), pltpu.VMEM((tm, D_FF), BF16)],
        compiler_params=_cparams(("parallel",)),
        name="conv_ffn",
    )(x, x, gain.reshape(1, D_MODEL), w_in, conv_w, conv_b.reshape(1, 2 * D_FF), w_out,
      final_gain.reshape(1, D_MODEL))


def _split3(x):
    hi = x.astype(BF16)
    r = x - hi.astype(F32)
    mid = r.astype(BF16)
    lo = (r - mid.astype(F32)).astype(BF16)
    return hi, mid, lo


def _fgate_body(af_ref, b_ref, c_ref):
    s = af_ref.shape[0]
    blk = 512
    ri = lax.broadcasted_iota(I32, (blk, blk), 0)
    ci = lax.broadcasted_iota(I32, (blk, blk), 1)
    tri = jnp.where(ci <= ri, 1.0, 0.0).astype(BF16)
    carry = jnp.zeros((1, LANES), F32)
    for r0 in range(0, s, blk):
        lf = jax.nn.log_sigmoid(af_ref[r0:r0 + blk, :] + b_ref[...])
        hi, mid, lo = _split3(lf)
        c = (jnp.dot(tri, hi, preferred_element_type=F32)
             + jnp.dot(tri, mid, preferred_element_type=F32)
             + jnp.dot(tri, lo, preferred_element_type=F32)) + carry
        c_ref[r0:r0 + blk, :] = c
        carry = c[blk - 1:blk, :]


def _fgate_cumsum(af, bias, batch, seq):
    return pl.pallas_call(
        _fgate_body,
        grid=(batch,),
        in_specs=[pl.BlockSpec((seq, LANES), lambda b: (b, 0)),
                  pl.BlockSpec((1, LANES), lambda b: (0, 0))],
        out_specs=pl.BlockSpec((seq, LANES), lambda b: (b, 0)),
        out_shape=jax.ShapeDtypeStruct((batch * seq, LANES), F32),
        compiler_params=_cparams(("parallel",)),
        name="fox_gate_cumsum",
    )(af, bias)


def _nt_dot(a, b):
    return lax.dot_general(a, b, (((1,), (1,)), ((), ())), preferred_element_type=F32)


def _per_query_block(qi, n, fn):
    for c in range(n):
        @pl.when(qi == c)
        def _():
            fn(c)


def _stack_heads(q):
    lane = lax.broadcasted_iota(I32, (1, LANES), 1)
    zero = jnp.zeros_like(q)
    return jnp.concatenate([jnp.where(lane < HEAD_DIM, q, zero), jnp.where(lane >= HEAD_DIM, q, zero)], axis=0)


def _unstack_heads(o):
    tq = o.shape[0] // 2
    lane = lax.broadcasted_iota(I32, (1, LANES), 1)
    return jnp.where(lane < HEAD_DIM, o[:tq], o[tq:])


def _diag_mask(rows, tq, strict):
    qpos = lax.broadcasted_iota(I32, (rows, tq), 0) & (tq - 1)
    kpos = lax.broadcasted_iota(I32, (rows, tq), 1)
    return kpos < qpos if strict else kpos <= qpos


def _fill_diag(s, tq, strict, fill):
    nk = s.shape[1]
    d = jnp.where(_diag_mask(s.shape[0], tq, strict), s[:, nk - tq:], fill)
    return d if nk == tq else jnp.concatenate([s[:, :nk - tq], d], axis=1)


def _fold_rows(x, op):
    while x.shape[0] > 8:
        half = x.shape[0] // 2
        x = op(x[:half], x[half:])
    return x


def _online_softmax_pv(rows, nk, score_chunk, v_ref):
    m = jnp.full((rows, 1), -jnp.inf, F32)
    l = jnp.zeros((rows, 1), F32)
    acc = jnp.zeros((rows, LANES), F32)
    for k0 in range(0, nk, ATT_CHUNK):
        k1 = min(k0 + ATT_CHUNK, nk)
        s = score_chunk(k0, k1)
        m_new = jnp.maximum(m, jnp.max(s, axis=-1, keepdims=True))
        p = jnp.exp(s - m_new)
        alpha = jnp.exp(m - m_new)
        l = alpha * l + jnp.sum(p, axis=-1, keepdims=True)
        acc = alpha * acc + jnp.dot(p.astype(BF16), v_ref[k0:k1, :], preferred_element_type=F32)
        m = m_new
    return acc / l


def _fox_block(c, q_ref, k_ref, v_ref, ccol_ref, crow_ref, o_ref):
    tq = ATT_TQ
    nk = (c + 1) * tq
    q_st = _stack_heads(q_ref[...])

    def score_chunk(k0, k1):
        s = _nt_dot(q_st, k_ref[k0:k1, :])
        bias = jnp.concatenate([ccol_ref[:, 0:1] - crow_ref[0:1, k0:k1],
                                ccol_ref[:, 1:2] - crow_ref[1:2, k0:k1]], axis=0)
        return _fill_diag(s + bias, tq, False, -jnp.inf) if k1 == nk else s + bias

    o_ref[...] = _unstack_heads(_online_softmax_pv(2 * tq, nk, score_chunk, v_ref)).astype(o_ref.dtype)


def _fox_body(*refs):
    _per_query_block(pl.program_id(2), refs[1].shape[0] // ATT_TQ, lambda c: _fox_block(c, *refs))


def _fox_attention(q, k, v, ccol, crow, batch, seq):
    tq = ATT_TQ
    nq = seq // tq
    pairs = q.shape[1] // LANES
    qmap = lambda b, p, i: (b * nq + i, p)
    kvmap = lambda b, p, i: (b, p)
    return pl.pallas_call(
        _fox_body,
        grid=(batch, pairs, nq),
        in_specs=[pl.BlockSpec((tq, LANES), qmap),
                  pl.BlockSpec((seq, LANES), kvmap),
                  pl.BlockSpec((seq, LANES), kvmap),
                  pl.BlockSpec((None, None, tq, 2), lambda b, p, i: (b, p, i, 0)),
                  pl.BlockSpec((None, None, 2, seq), lambda b, p, i: (b, p, 0, 0))],
        out_specs=pl.BlockSpec((tq, LANES), qmap),
        out_shape=jax.ShapeDtypeStruct(q.shape, BF16),
        compiler_params=_cparams(("parallel", "parallel", "arbitrary")),
        name="fox_attention",
    )(q, k, v, ccol, crow)


def _diff_block(c, lam_init, q_ref, k_ref, v_ref, lq1_ref, lk1_ref, lq2_ref, lk2_ref, sub_ref, o_ref):
    tq = ATT_TQ
    nk = (c + 1) * tq
    q_st = _stack_heads(q_ref[...])

    def score_chunk(k0, k1):
        s = _nt_dot(q_st, k_ref[k0:k1, :])
        return _fill_diag(s, tq, False, -jnp.inf) if k1 == nk else s

    o = _online_softmax_pv(2 * tq, nk, score_chunk, v_ref)
    lam = (jnp.exp(jnp.sum(lq1_ref[...] * lk1_ref[...], axis=-1, keepdims=True))
           - jnp.exp(jnp.sum(lq2_ref[...] * lk2_ref[...], axis=-1, keepdims=True)) + lam_init)
    o = o[:tq] - lam * o[tq:]
    o_ref[...] = (_rms_rows(o, sub_ref[...]) * (1.0 - lam_init)).astype(o_ref.dtype)


def _diff_body(lam_init, *refs):
    _per_query_block(pl.program_id(2), refs[1].shape[0] // ATT_TQ, lambda c: _diff_block(c, lam_init, *refs))


def _diff_attention(q, k, v, lq1, lk1, lq2, lk2, subln, lam_init, batch, seq):
    tq = ATT_TQ
    nq = seq // tq
    heads = q.shape[1] // LANES
    qmap = lambda b, h, i: (b * nq + i, h)
    kvmap = lambda b, h, i: (b, h)
    vec = lambda n: pl.BlockSpec((1, n), lambda b, h, i: (0, 0))
    return pl.pallas_call(
        functools.partial(_diff_body, lam_init),
        grid=(batch, heads, nq),
        in_specs=[pl.BlockSpec((tq, LANES), qmap),
                  pl.BlockSpec((seq, LANES), kvmap),
                  pl.BlockSpec((seq, LANES), kvmap),
                  vec(HEAD_DIM), vec(HEAD_DIM), vec(HEAD_DIM), vec(HEAD_DIM), vec(LANES)],
        out_specs=pl.BlockSpec((tq, LANES), qmap),
        out_shape=jax.ShapeDtypeStruct(q.shape, BF16),
        compiler_params=_cparams(("parallel", "parallel", "arbitrary")),
        name="diff_attention",
    )(q, k, v, lq1.reshape(1, -1), lk1.reshape(1, -1), lq2.reshape(1, -1), lk2.reshape(1, -1),
      subln.reshape(1, -1))


def _sb_block(c, q_ref, k_ref, v_ref, o_ref):
    tq = tk = SB_TQ
    nk = (c + 1) * tq
    z = _nt_dot(_stack_heads(q_ref[...]), k_ref[0:nk, :])
    log1m = -jnp.maximum(z, 0.0) - jnp.log(1.0 + jnp.exp(-jnp.abs(z)))
    log1m = _fill_diag(log1m, tq, True, 0.0)
    ri = lax.broadcasted_iota(I32, (2 * tk, tk), 0) & (tk - 1)
    ci = lax.broadcasted_iota(I32, (2 * tk, tk), 1)
    later = jnp.where(ri > ci, 1.0, 0.0).astype(BF16)
    run = jnp.zeros((2 * tq, 1), F32)
    acc = jnp.zeros((2 * tq, LANES), F32)
    for blk in reversed(range(nk // tk)):
        lb = log1m[:, blk * tk:(blk + 1) * tk]
        hi = lb.astype(BF16)
        lo = (lb - hi.astype(F32)).astype(BF16)
        local = jnp.dot(jnp.concatenate([hi, lo], axis=1), later, preferred_element_type=F32)
        a = jnp.exp(z[:, blk * tk:(blk + 1) * tk] + lb + (local + run))
        if blk == c:
            a = jnp.where(_diag_mask(2 * tq, tq, True), a, 0.0)
        acc = acc + jnp.dot(a.astype(BF16), v_ref[blk * tk:(blk + 1) * tk, :], preferred_element_type=F32)
        run = run + (local[:, 0:1] + lb[:, 0:1])
    o_ref[...] = _unstack_heads(acc).astype(o_ref.dtype)


def _sb_body(*refs):
    _per_query_block(pl.program_id(2), refs[1].shape[0] // SB_TQ, lambda c: _sb_block(c, *refs))


def _sb_attention(q, k, v, batch, seq):
    tq = SB_TQ
    nq = seq // tq
    pairs = q.shape[1] // LANES
    qmap = lambda b, p, i: (b * nq + i, p)
    kvmap = lambda b, p, i: (b, p)
    return pl.pallas_call(
        _sb_body,
        grid=(batch, pairs, nq),
        in_specs=[pl.BlockSpec((tq, LANES), qmap),
                  pl.BlockSpec((seq, LANES), kvmap),
                  pl.BlockSpec((seq, LANES), kvmap)],
        out_specs=pl.BlockSpec((tq, LANES), qmap),
        out_shape=jax.ShapeDtypeStruct(q.shape, BF16),
        compiler_params=_cparams(("parallel", "parallel", "arbitrary")),
        name="stickbreak_attention",
    )(q, k, v)


def _dsa_body(top_k, iq_ref, iwt_ref, cq_ref, ika_ref, ikb_ref, kv_ref, ckvt_ref, mq_ref, wuvt_ref,
              o_ref, key_scr, sel_scr, qcat_scr, cut_scr, m_scr, l_scr, acc_scr, s_scr):
    tq, kc = DSA_TQ, DSA_KC
    qi = pl.program_id(1)
    n_chunks = qi + 1
    int_min = jnp.int32(INT_MIN)
    qpos = qi * tq + lax.broadcasted_iota(I32, (kc, tq), 1)
    krow = lax.broadcasted_iota(I32, (kc, tq), 0)
    group = DSA_COUNT_GROUP * kc
    grow = lax.broadcasted_iota(I32, (group, tq), 0)

    def over_chunks(fn, init):
        return lax.fori_loop(0, n_chunks, fn, init)

    def key_rows(j):
        return j // DSA_COUNT_GROUP, pl.ds(pl.multiple_of((j % DSA_COUNT_GROUP) * kc, kc), kc)

    def count_keys(pred_of_group):
        def add(g, acc):
            ones = jnp.where(pred_of_group(g * group, key_scr[g]), 1.0, 0.0)
            return acc + _fold_rows(ones, jnp.add)
        acc = lax.fori_loop(0, qi // DSA_COUNT_GROUP + 1, add, jnp.zeros((8, tq), F32))
        return jnp.sum(acc, axis=0, keepdims=True)

    iq = iq_ref[...]
    iq_st = jnp.concatenate([iq[:, :LANES], iq[:, LANES:]], axis=0)

    def build_keys(j, carry):
        k0 = pl.multiple_of(j * kc, kc)
        la = jnp.maximum(_nt_dot(ika_ref[pl.ds(k0, kc), :], iq_st), 0.0)
        lb = jnp.maximum(_nt_dot(ikb_ref[pl.ds(k0, kc), :], iq_st), 0.0)
        score = (la[:, :tq] * iwt_ref[0:1, :] + lb[:, :tq] * iwt_ref[1:2, :]
                 + la[:, tq:] * iwt_ref[2:3, :] + lb[:, tq:] * iwt_ref[3:4, :])
        bits = pltpu.bitcast(score, I32)
        keys = bits ^ ((bits >> 31) & jnp.int32(0x7FFFFFFF))
        keys = jnp.where(bits == int_min, 0, keys)
        g, rows = key_rows(j)
        key_scr[g, rows, :] = jnp.where(k0 + krow <= qpos, keys, int_min)
        return carry

    key_scr[qi // DSA_COUNT_GROUP] = jnp.full((group, tq), INT_MIN, I32)
    over_chunks(build_keys, 0)

    def value_bit(i, tu):
        cand = tu | lax.shift_left(jnp.int32(1), 31 - i)
        cnt = count_keys(lambda k0, keys: keys >= (cand ^ int_min))
        return jnp.where(cnt >= top_k, cand, tu)

    thr = lax.fori_loop(0, 32, value_bit, jnp.zeros((1, tq), I32)) ^ int_min
    need = top_k - count_keys(lambda k0, keys: keys > thr)

    excess = jnp.where(thr == int_min, 0.0, count_keys(lambda k0, keys: keys == thr) - need)
    index_bits = (kv_ref.shape[0] - 1).bit_length()
    cut_scr[...] = jnp.full(cut_scr.shape, 1 << index_bits, I32)

    @pl.when(jnp.max(excess) > 0.0)
    def _():
        def index_bit(i, x):
            cand = x | lax.shift_left(jnp.int32(1), index_bits - 1 - i)
            cnt = count_keys(lambda k0, keys: (keys == thr) & (k0 + grow < cand))
            return jnp.where(cnt < need, cand, x)

        cut = lax.fori_loop(0, index_bits, index_bit, jnp.zeros((1, tq), I32))
        cut_scr[...] = jnp.broadcast_to(cut, cut_scr.shape)

    cut = cut_scr[0:1, :]

    def build_mask(j, carry):
        g, rows = key_rows(j)
        keys = key_scr[g, rows, :]
        kpos = j * kc + krow
        chosen = ((keys > thr) | ((keys == thr) & (kpos <= cut))) & (kpos <= qpos)
        sel_scr[j] = jnp.where(chosen, 0.0, -jnp.inf)
        return carry

    over_chunks(build_mask, 0)

    for p in range(DSA_HEADS // 2):
        t = jnp.dot(cq_ref[p], mq_ref[p], preferred_element_type=F32)
        qcat_scr[(2 * p) * tq:(2 * p + 1) * tq, :] = t[:, :2 * LANES].astype(BF16)
        qcat_scr[(2 * p + 1) * tq:(2 * p + 2) * tq, :] = t[:, 2 * LANES:].astype(BF16)

    rows = DSA_HEADS * tq

    m_scr[...] = jnp.full(m_scr.shape, -jnp.inf, F32)
    l_scr[...] = jnp.zeros(l_scr.shape, F32)
    acc_scr[...] = jnp.zeros(acc_scr.shape, F32)

    def scores(j):
        k0 = pl.multiple_of(j * kc, kc)
        return _nt_dot(kv_ref[pl.ds(k0, kc), :], qcat_scr[...])

    s_scr[...] = scores(0)

    def attend(j, carry):
        s = s_scr[...]
        s_scr[...] = scores(jnp.minimum(j + 1, qi))
        s = s + jnp.concatenate([sel_scr[j]] * DSA_HEADS, axis=1)
        m = m_scr[...]
        m_new = jnp.maximum(m, jnp.max(_fold_rows(s, jnp.maximum), axis=0, keepdims=True))
        m_safe = jnp.where(m_new == -jnp.inf, 0.0, m_new)
        p = jnp.exp(s - m_safe)
        alpha = jnp.exp(m - m_safe)
        m_scr[...] = m_new
        l_scr[...] = alpha * l_scr[...] + jnp.sum(_fold_rows(p, jnp.add), axis=0, keepdims=True)
        acc_scr[...] = alpha * acc_scr[...] + jnp.dot(ckvt_ref[j], p.astype(BF16), preferred_element_type=F32)
        return carry

    over_chunks(attend, 0)
    o_lat = (acc_scr[...] / l_scr[...]).astype(BF16)
    for p in range(DSA_HEADS // 2):
        pair = jnp.concatenate([o_lat[:, (2 * p) * tq:(2 * p + 1) * tq],
                                o_lat[:, (2 * p + 1) * tq:(2 * p + 2) * tq]], axis=0)
        out_t = jnp.dot(wuvt_ref[p], pair, preferred_element_type=F32)
        o_ref[p] = out_t.T.astype(o_ref.dtype)


def _dsa_attention(iq, iwt, cq, ika, ikb, kv, ckvt, mq, wuvt, top_k, batch, seq):
    tq, kc = DSA_TQ, DSA_KC
    nq = seq // tq
    qmap = lambda b, i: (b * nq + i, 0)
    pmap = lambda b, i: (0, b * nq + i, 0)
    kmap = lambda b, i: (b, 0)
    cmap = lambda b, i: (0, 0, 0)
    pairs = cq.shape[0]
    return pl.pallas_call(
        functools.partial(_dsa_body, top_k),
        grid=(batch, nq),
        in_specs=[pl.BlockSpec((tq, iq.shape[1]), qmap),
                  pl.BlockSpec((IDX_HEADS, tq), lambda b, i: (0, b * nq + i)),
                  pl.BlockSpec((pairs, tq, LANES), pmap),
                  pl.BlockSpec((seq, LANES), kmap),
                  pl.BlockSpec((seq, LANES), kmap),
                  pl.BlockSpec((seq, 2 * LANES), kmap),
                  pl.BlockSpec((None, seq // kc, DSA_KV_RANK, kc), lambda b, i: (b, 0, 0, 0)),
                  pl.BlockSpec(mq.shape, cmap),
                  pl.BlockSpec(wuvt.shape, cmap)],
        out_specs=pl.BlockSpec((pairs, tq, LANES), pmap),
        out_shape=jax.ShapeDtypeStruct(cq.shape, BF16),
        scratch_shapes=[pltpu.VMEM((seq // (DSA_COUNT_GROUP * kc), DSA_COUNT_GROUP * kc, tq), I32),
                        pltpu.VMEM((seq // kc, kc, tq), F32),
                        pltpu.VMEM((DSA_HEADS * tq, 2 * LANES), BF16), pltpu.VMEM((8, tq), I32),
                        pltpu.VMEM((1, DSA_HEADS * tq), F32), pltpu.VMEM((1, DSA_HEADS * tq), F32),
                        pltpu.VMEM((DSA_KV_RANK, DSA_HEADS * tq), F32), pltpu.VMEM((kc, DSA_HEADS * tq), F32)],
        compiler_params=_cparams(("parallel", "arbitrary")),
        name="dsa_attention",
    )(iq, iwt, cq, ika, ikb, kv, ckvt, mq, wuvt)


def _rope_tables(positions):
    inv_freq = ROPE_THETA ** (-jnp.arange(0, ROT_DIM, 2, dtype=F32) / ROT_DIM)
    ang = positions.astype(F32).reshape(-1, 1) * inv_freq
    cos, sin = jnp.cos(ang), jnp.sin(ang)
    t = ang.shape[0]
    pad = jnp.zeros((t, HEAD_DIM - ROT_DIM), F32)
    zero8 = jnp.zeros((t, ROT_HALF), F32)
    c64 = jnp.concatenate([cos, cos, pad + 1.0], axis=1)
    sa64 = jnp.concatenate([-sin, zero8, pad], axis=1)
    sb64 = jnp.concatenate([zero8, sin, pad], axis=1)
    return tuple(jnp.tile(a, (1, LANES // HEAD_DIM)) for a in (c64, sa64, sb64))


def _pad_cols(w, width):
    return jnp.pad(w, ((0, 0), (0, width - w.shape[1])))


EVEN_GROUPS = ((FOX_WIDTH, "plain", False), (FOX_WIDTH, "plain", False), (FOX_WIDTH, "plain", False),
               (DIFF_WIDTH, "rope", False), (DIFF_WIDTH, "rope", False), (DIFF_WIDTH, "plain", False),
               (LANES, "f32", False))
ODD_GROUPS = ((DSA_WIDTH, "rope", True), (IDX_HEADS * IDX_DIM, "rope", False), (LANES, "rope", False),
              (LANES, "rope", False), (2 * LANES, "kv", False),
              (SB_WIDTH, "plain", False), (SB_WIDTH, "plain", False), (SB_WIDTH, "plain", False),
              (LANES, "f32", False))


def _pack_even(w):
    aq, ak, av, af, bq, bk, bv = jnp.split(w, (512, 1024, 1536, 1544, 2056, 2568), axis=1)
    return jnp.concatenate([aq * QK_SCALE, ak, av, bq * QK_SCALE, bk, bv, _pad_cols(af, LANES)],
                           axis=1).astype(BF16)


def _pack_odd(w):
    cq, ckv, ckr, ciq, ciw, cik, sq, sk, sv = jnp.split(w, (512, 640, 656, 912, 916, 980, 1492, 2004), axis=1)
    zeros64 = jnp.zeros_like(cik)
    return jnp.concatenate([cq * QK_SCALE, ciq,
                            jnp.concatenate([cik, zeros64], axis=1), jnp.concatenate([zeros64, cik], axis=1),
                            ckv, _pad_cols(ckr, LANES),
                            sq * QK_SCALE, sk, sv, _pad_cols(ciw * (IDX_HEADS ** -0.5), LANES)],
                           axis=1).astype(BF16)


def _pack_dsa_weights(w_uk, w_uv):
    mq, wuv = [], []
    eye = jnp.eye(ROT_DIM, dtype=F32)
    for p in range(DSA_HEADS // 2):
        m = jnp.zeros((LANES, 4 * LANES), F32)
        u = jnp.zeros((2 * DSA_KV_RANK, LANES), F32)
        for hh in range(2):
            h = 2 * p + hh
            r0 = hh * HEAD_DIM
            c0 = hh * 2 * LANES
            m = m.at[r0 + ROT_DIM:r0 + HEAD_DIM, c0:c0 + DSA_KV_RANK].set(w_uk[:, h, :].T)
            m = m.at[r0:r0 + ROT_DIM, c0 + DSA_KV_RANK:c0 + DSA_KV_RANK + ROT_DIM].set(eye)
            u = u.at[hh * DSA_KV_RANK:(hh + 1) * DSA_KV_RANK, r0:r0 + HEAD_DIM].set(w_uv[:, h, :])
        mq.append(m)
        wuv.append(u)
    return jnp.stack(mq).astype(BF16), jnp.stack(wuv).astype(BF16)


def kernel(x, positions, attn_norm, ffn_norm, final_norm, ev_w_in, ev_fgate_b, ev_lambda_q1, ev_lambda_k1,
           ev_lambda_q2, ev_lambda_k2, ev_subln, ev_w_out, od_w_in, od_kv_norm, od_w_uk, od_w_uv, od_w_out,
           ffn_w_in, ffn_conv_w, ffn_conv_b, ffn_w_out):
    batch, seq, _ = x.shape
    depth = attn_norm.shape[0]
    t = batch * seq
    top_k = min(IDX_TOPK_MAX, seq // 4)
    tabs = _rope_tables(positions)
    xf = x.reshape(t, D_MODEL)
    ones_kv = jnp.ones((LANES,), F32)
    for layer in range(depth):
        j = layer // 2
        if layer % 2 == 0:
            aq, ak, av, bq, bk, bv, af = _in_proj(xf, attn_norm[layer], _pack_even(ev_w_in[j]), tabs, ones_kv,
                                                  EVEN_GROUPS)
            c = _fgate_cumsum(af, _pad_cols(ev_fgate_b[j].reshape(1, -1), LANES), batch, seq)
            c8 = c.reshape(batch, seq, LANES)[:, :, :FOX_WIDTH // HEAD_DIM]
            ccol = c8.reshape(batch, seq, -1, 2).transpose(0, 2, 1, 3)
            crow = c8.reshape(batch, seq, -1, 2).transpose(0, 2, 3, 1)
            o_a = _fox_attention(aq, ak, av, ccol, crow, batch, seq)
            lam_init = 0.8 - 0.6 * math.exp(-0.3 * layer)
            o_b = _diff_attention(bq, bk, bv, ev_lambda_q1[j], ev_lambda_k1[j], ev_lambda_q2[j], ev_lambda_k2[j],
                                  ev_subln[j], lam_init, batch, seq)
            xf = _out_proj(xf, o_a, o_b, ev_w_out[j].astype(BF16))
        else:
            cq, iq, ika, ikb, kv, sq, sk, sv, iw = _in_proj(xf, attn_norm[layer], _pack_odd(od_w_in[j]), tabs,
                                                            od_kv_norm[j], ODD_GROUPS)
            mq, wuv = _pack_dsa_weights(od_w_uk[j], od_w_uv[j])
            ckvt = kv[:, :DSA_KV_RANK].reshape(batch, seq // DSA_KC, DSA_KC, DSA_KV_RANK).transpose(0, 1, 3, 2)
            o_c = _dsa_attention(iq, iw[:, :IDX_HEADS].T, cq, ika, ikb, kv, ckvt, mq, wuv.transpose(0, 2, 1),
                                 top_k, batch, seq)
            o_d = _sb_attention(sq, sk, sv, batch, seq)
            xf = _out_proj(xf, o_c, o_d, od_w_out[j].astype(BF16))
        xf = _ffn(xf, ffn_norm[layer], ffn_w_in[layer].astype(BF16), ffn_conv_w[layer], ffn_conv_b[layer],
                  ffn_w_out[layer].astype(BF16), final_norm, seq, layer == depth - 1)
    return xf.reshape(batch, seq, D_MODEL)
```

```python
import functools
import math

import jax
import jax.numpy as jnp
from jax import lax
from jax.experimental import pallas as pl
from jax.experimental.pallas import tpu as pltpu

F32 = jnp.float32
BF16 = jnp.bfloat16
I32 = jnp.int32

D_MODEL = 1024
HEAD_DIM = 64
ROT_DIM = 16
ROT_HALF = 8
ROPE_THETA = 500000.0
EPS = 1e-6
LANES = 128
QK_SCALE = HEAD_DIM ** -0.5
FOX_WIDTH = 512
DIFF_WIDTH = 512
DIFF_HEADS = 4
DSA_WIDTH = 512
DSA_HEADS = 8
DSA_NOPE = HEAD_DIM - ROT_DIM
DSA_KV_RANK = 128
IDX_HEADS = 4
IDX_DIM = 64
IDX_TOPK_MAX = 256
SB_WIDTH = 512
D_FF = 2816
INT_MIN = -(2 ** 31)

VMEM_LIMIT = 56 * 1024 * 1024

PROJ_TM = 512
PROJ_CHUNK = 256
FFN_TM = 512
FFN_HALO = 16
FFN_CHUNK = 256
ATT_TQ = 512
SB_TQ = 256
ATT_CHUNK = 1024
DSA_TQ = 256
DSA_KC = 256
DSA_COUNT_GROUP = 1


def _cparams(sem):
    return pltpu.CompilerParams(dimension_semantics=sem, vmem_limit_bytes=VMEM_LIMIT)


def _rms_rows(x, g):
    ms = jnp.mean(x * x, axis=-1, keepdims=True)
    return x * lax.rsqrt(ms + EPS) * g


def _rope128(z, c, sa, sb):
    return z * c + pltpu.roll(z, LANES - ROT_HALF, axis=1) * sa + pltpu.roll(z, ROT_HALF, axis=1) * sb


def _in_proj_body(groups, x_ref, g_ref, w_ref, c_ref, sa_ref, sb_ref, kvg_ref, *rest):
    out_refs, h_scr = rest[:-1], rest[-1]
    h_scr[...] = _rms_rows(x_ref[...], g_ref[...]).astype(BF16)
    col = 0
    for (width, kind, block_major), o_ref in zip(groups, out_refs):
        cw = min(width, PROJ_CHUNK)
        for c0 in range(0, width, cw):
            zc = jnp.dot(h_scr[...], w_ref[:, col + c0:col + c0 + cw], preferred_element_type=F32)
            for l0 in range(0, cw, LANES):
                z = zc[:, l0:l0 + LANES]
                if kind == "rope" or (kind == "kv" and c0 + l0 == DSA_KV_RANK):
                    z = _rope128(z, c_ref[...], sa_ref[...], sb_ref[...])
                elif kind == "kv":
                    z = _rms_rows(z, kvg_ref[...])
                if block_major:
                    o_ref[(c0 + l0) // LANES] = z.astype(o_ref.dtype)
                else:
                    o_ref[:, c0 + l0:c0 + l0 + LANES] = z.astype(o_ref.dtype)
        col += width


def _in_proj(x, gain, w_cat, tabs, kv_gain, groups):
    t = x.shape[0]
    n = w_cat.shape[1]
    tm = PROJ_TM
    row = lambda i: (i, 0)
    const = lambda i: (0, 0)
    out_shape, out_specs = [], []
    for w, kind, block_major in groups:
        dtype = F32 if kind == "f32" else BF16
        if block_major:
            out_shape.append(jax.ShapeDtypeStruct((w // LANES, t, LANES), dtype))
            out_specs.append(pl.BlockSpec((w // LANES, tm, LANES), lambda i: (0, i, 0)))
        else:
            out_shape.append(jax.ShapeDtypeStruct((t, w), dtype))
            out_specs.append(pl.BlockSpec((tm, w), row))
    return pl.pallas_call(
        functools.partial(_in_proj_body, groups),
        grid=(t // tm,),
        in_specs=[pl.BlockSpec((tm, D_MODEL), row),
                  pl.BlockSpec((1, D_MODEL), const),
                  pl.BlockSpec((D_MODEL, n), const),
                  pl.BlockSpec((tm, LANES), row),
                  pl.BlockSpec((tm, LANES), row),
                  pl.BlockSpec((tm, LANES), row),
                  pl.BlockSpec((1, LANES), const)],
        out_specs=out_specs,
        out_shape=out_shape,
        scratch_shapes=[pltpu.VMEM((tm, D_MODEL), BF16)],
        compiler_params=_cparams(("parallel",)),
        name="in_proj",
    )(x, gain.reshape(1, D_MODEL), w_cat, tabs[0], tabs[1], tabs[2], kv_gain.reshape(1, LANES))


def _out_proj_body(x_ref, a_ref, b_ref, w_ref, o_ref):
    if len(a_ref.shape) == 3:
        a = jnp.concatenate([a_ref[i] for i in range(a_ref.shape[0])], axis=1)
    else:
        a = a_ref[...]
    half = a.shape[1]
    y = jnp.dot(a, w_ref[:half, :], preferred_element_type=F32)
    y = y + jnp.dot(b_ref[...], w_ref[half:, :], preferred_element_type=F32)
    o_ref[...] = x_ref[...] + y


def _out_proj(x, a, b, w):
    t = x.shape[0]
    tm = PROJ_TM
    row = lambda i: (i, 0)
    const = lambda i: (0, 0)
    a_spec = (pl.BlockSpec((a.shape[0], tm, LANES), lambda i: (0, i, 0)) if a.ndim == 3
              else pl.BlockSpec((tm, a.shape[1]), row))
    return pl.pallas_call(
        _out_proj_body,
        grid=(t // tm,),
        in_specs=[pl.BlockSpec((tm, D_MODEL), row),
                  a_spec,
                  pl.BlockSpec((tm, b.shape[1]), row),
                  pl.BlockSpec((D_MODEL, D_MODEL), const)],
        out_specs=pl.BlockSpec((tm, D_MODEL), row),
        out_shape=jax.ShapeDtypeStruct((t, D_MODEL), F32),
        compiler_params=_cparams(("parallel",)),
        name="out_proj",
    )(x, a, b, w)


def _ffn_body(tiles_per_seq, final, x_ref, xh_ref, g_ref, win_ref, cw_ref, cb_ref, wout_ref, fg_ref,
              o_ref, h_scr, act_scr):
    tm = x_ref.shape[0]
    x = x_ref[...]
    g = g_ref[...]
    keep = (pl.program_id(0) % tiles_per_seq != 0).astype(F32)
    h_scr[:FFN_HALO, :] = (_rms_rows(xh_ref[...], g) * keep).astype(BF16)
    h_scr[FFN_HALO:, :] = _rms_rows(x, g).astype(BF16)

    def conv(u, c0):
        y = (pltpu.roll(u, 2, axis=0) * cw_ref[0:1, c0:c0 + FFN_CHUNK]
             + pltpu.roll(u, 1, axis=0) * cw_ref[1:2, c0:c0 + FFN_CHUNK]
             + u * cw_ref[2:3, c0:c0 + FFN_CHUNK])
        return y[FFN_HALO:, :] + cb_ref[0:1, c0:c0 + FFN_CHUNK]

    for ci in range(D_FF // FFN_CHUNK):
        cg = ci * FFN_CHUNK
        cv = D_FF + cg
        ug = jnp.dot(h_scr[...], win_ref[:, cg:cg + FFN_CHUNK], preferred_element_type=F32)
        uv = jnp.dot(h_scr[...], win_ref[:, cv:cv + FFN_CHUNK], preferred_element_type=F32)
        gate = conv(ug, cg)
        val = conv(uv, cv)
        act_scr[:, cg:cg + FFN_CHUNK] = (gate * jax.nn.sigmoid(gate) * val).astype(BF16)
    y = x + jnp.dot(act_scr[...], wout_ref[...], preferred_element_type=F32)
    if final:
        y = _rms_rows(y, fg_ref[...])
    o_ref[...] = y


def _ffn(x, gain, w_in, conv_w, conv_b, w_out, final_gain, seq, final):
    t = x.shape[0]
    tm = FFN_TM
    halo_blocks = tm // FFN_HALO
    row = lambda i: (i, 0)
    const = lambda i: (0, 0)
    return pl.pallas_call(
        functools.partial(_ffn_body, seq // tm, final),
        grid=(t // tm,),
        in_specs=[pl.BlockSpec((tm, D_MODEL), row),
                  pl.BlockSpec((FFN_HALO, D_MODEL), lambda i: (jnp.maximum(i * halo_blocks - 1, 0), 0)),
                  pl.BlockSpec((1, D_MODEL), const),
                  pl.BlockSpec((D_MODEL, 2 * D_FF), const),
                  pl.BlockSpec((3, 2 * D_FF), const),
                  pl.BlockSpec((1, 2 * D_FF), const),
                  pl.BlockSpec((D_FF, D_MODEL), const),
                  pl.BlockSpec((1, D_MODEL), const)],
        out_specs=pl.BlockSpec((tm, D_MODEL), row),
        out_shape=jax.ShapeDtypeStruct((t, D_MODEL), F32),
        scratch_shapes=[pltpu.VMEM((tm + FFN_HALO, D_MODEL), BF16), pltpu.VMEM((tm, D_FF), BF16)],
        compiler_params=_cparams(("parallel",)),
        name="conv_ffn",
    )(x, x, gain.reshape(1, D_MODEL), w_in, conv_w, conv_b.reshape(1, 2 * D_FF), w_out,
      final_gain.reshape(1, D_MODEL))


def _split3(x):
    hi = x.astype(BF16)
    r = x - hi.astype(F32)
    mid = r.astype(BF16)
    lo = (r - mid.astype(F32)).astype(BF16)
    return hi, mid, lo


def _fgate_body(af_ref, b_ref, c_ref):
    s = af_ref.shape[0]
    blk = 512
    ri = lax.broadcasted_iota(I32, (blk, blk), 0)
    ci = lax.broadcasted_iota(I32, (blk, blk), 1)
    tri = jnp.where(ci <= ri, 1.0, 0.0).astype(BF16)
    carry = jnp.zeros((1, LANES), F32)
    for r0 in range(0, s, blk):
        lf = jax.nn.log_sigmoid(af_ref[r0:r0 + blk, :] + b_ref[...])
        hi, mid, lo = _split3(lf)
        c = (jnp.dot(tri, hi, preferred_element_type=F32)
             + jnp.dot(tri, mid, preferred_element_type=F32)
             + jnp.dot(tri, lo, preferred_element_type=F32)) + carry
        c_ref[r0:r0 + blk, :] = c
        carry = c[blk - 1:blk, :]


def _fgate_cumsum(af, bias, batch, seq):
    return pl.pallas_call(
        _fgate_body,
        grid=(batch,),
        in_specs=[pl.BlockSpec((seq, LANES), lambda b: (b, 0)),
                  pl.BlockSpec((1, LANES), lambda b: (0, 0))],
        out_specs=pl.BlockSpec((seq, LANES), lambda b: (b, 0)),
        out_shape=jax.ShapeDtypeStruct((batch * seq, LANES), F32),
        compiler_params=_cparams(("parallel",)),
        name="fox_gate_cumsum",
    )(af, bias)


def _nt_dot(a, b):
    return lax.dot_general(a, b, (((1,), (1,)), ((), ())), preferred_element_type=F32)


def _per_query_block(qi, n, fn):
    for c in range(n):
        @pl.when(qi == c)
        def _():
            fn(c)


def _stack_heads(q):
    lane = lax.broadcasted_iota(I32, (1, LANES), 1)
    zero = jnp.zeros_like(q)
    return jnp.concatenate([jnp.where(lane < HEAD_DIM, q, zero), jnp.where(lane >= HEAD_DIM, q, zero)], axis=0)


def _unstack_heads(o):
    tq = o.shape[0] // 2
    lane = lax.broadcasted_iota(I32, (1, LANES), 1)
    return jnp.where(lane < HEAD_DIM, o[:tq], o[tq:])


def _diag_mask(rows, tq, strict):
    qpos = lax.broadcasted_iota(I32, (rows, tq), 0) & (tq - 1)
    kpos = lax.broadcasted_iota(I32, (rows, tq), 1)
    return kpos < qpos if strict else kpos <= qpos


def _fill_diag(s, tq, strict, fill):
    nk = s.shape[1]
    d = jnp.where(_diag_mask(s.shape[0], tq, strict), s[:, nk - tq:], fill)
    return d if nk == tq else jnp.concatenate([s[:, :nk - tq], d], axis=1)


def _fold_rows(x, op):
    while x.shape[0] > 8:
        half = x.shape[0] // 2
        x = op(x[:half], x[half:])
    return x


def _online_softmax_pv(rows, nk, score_chunk, v_ref):
    m = jnp.full((rows, 1), -jnp.inf, F32)
    l = jnp.zeros((rows, 1), F32)
    acc = jnp.zeros((rows, LANES), F32)
    for k0 in range(0, nk, ATT_CHUNK):
        k1 = min(k0 + ATT_CHUNK, nk)
        s = score_chunk(k0, k1)
        m_new = jnp.maximum(m, jnp.max(s, axis=-1, keepdims=True))
        p = jnp.exp(s - m_new)
        alpha = jnp.exp(m - m_new)
        l = alpha * l + jnp.sum(p, axis=-1, keepdims=True)
        acc = alpha * acc + jnp.dot(p.astype(BF16), v_ref[k0:k1, :], preferred_element_type=F32)
        m = m_new
    return acc / l


def _fox_block(c, q_ref, k_ref, v_ref, ccol_ref, crow_ref, o_ref):
    tq = ATT_TQ
    nk = (c + 1) * tq
    q_st = _stack_heads(q_ref[...])

    def score_chunk(k0, k1):
        s = _nt_dot(q_st, k_ref[k0:k1, :])
        bias = jnp.concatenate([ccol_ref[:, 0:1] - crow_ref[0:1, k0:k1],
                                ccol_ref[:, 1:2] - crow_ref[1:2, k0:k1]], axis=0)
        return _fill_diag(s + bias, tq, False, -jnp.inf) if k1 == nk else s + bias

    o_ref[...] = _unstack_heads(_online_softmax_pv(2 * tq, nk, score_chunk, v_ref)).astype(o_ref.dtype)


def _fox_body(*refs):
    _per_query_block(pl.program_id(2), refs[1].shape[0] // ATT_TQ, lambda c: _fox_block(c, *refs))


def _fox_attention(q, k, v, ccol, crow, batch, seq):
    tq = ATT_TQ
    nq = seq // tq
    pairs = q.shape[1] // LANES
    qmap = lambda b, p, i: (b * nq + i, p)
    kvmap = lambda b, p, i: (b, p)
    return pl.pallas_call(
        _fox_body,
        grid=(batch, pairs, nq),
        in_specs=[pl.BlockSpec((tq, LANES), qmap),
                  pl.BlockSpec((seq, LANES), kvmap),
                  pl.BlockSpec((seq, LANES), kvmap),
                  pl.BlockSpec((None, None, tq, 2), lambda b, p, i: (b, p, i, 0)),
                  pl.BlockSpec((None, None, 2, seq), lambda b, p, i: (b, p, 0, 0))],
        out_specs=pl.BlockSpec((tq, LANES), qmap),
        out_shape=jax.ShapeDtypeStruct(q.shape, BF16),
        compiler_params=_cparams(("parallel", "parallel", "arbitrary")),
        name="fox_attention",
    )(q, k, v, ccol, crow)


def _diff_block(c, lam_init, q_ref, k_ref, v_ref, lq1_ref, lk1_ref, lq2_ref, lk2_ref, sub_ref, o_ref):
    tq = ATT_TQ
    nk = (c + 1) * tq
    q_st = _stack_heads(q_ref[...])

    def score_chunk(k0, k1):
        s = _nt_dot(q_st, k_ref[k0:k1, :])
        return _fill_diag(s, tq, False, -jnp.inf) if k1 == nk else s

    o = _online_softmax_pv(2 * tq, nk, score_chunk, v_ref)
    lam = (jnp.exp(jnp.sum(lq1_ref[...] * lk1_ref[...], axis=-1, keepdims=True))
           - jnp.exp(jnp.sum(lq2_ref[...] * lk2_ref[...], axis=-1, keepdims=True)) + lam_init)
    o = o[:tq] - lam * o[tq:]
    o_ref[...] = (_rms_rows(o, sub_ref[...]) * (1.0 - lam_init)).astype(o_ref.dtype)


def _diff_body(lam_init, *refs):
    _per_query_block(pl.program_id(2), refs[1].shape[0] // ATT_TQ, lambda c: _diff_block(c, lam_init, *refs))


def _diff_attention(q, k, v, lq1, lk1, lq2, lk2, subln, lam_init, batch, seq):
    tq = ATT_TQ
    nq = seq // tq
    heads = q.shape[1] // LANES
    qmap = lambda b, h, i: (b * nq + i, h)
    kvmap = lambda b, h, i: (b, h)
    vec = lambda n: pl.BlockSpec((1, n), lambda b, h, i: (0, 0))
    return pl.pallas_call(
        functools.partial(_diff_body, lam_init),
        grid=(batch, heads, nq),
        in_specs=[pl.BlockSpec((tq, LANES), qmap),
                  pl.BlockSpec((seq, LANES), kvmap),
                  pl.BlockSpec((seq, LANES), kvmap),
                  vec(HEAD_DIM), vec(HEAD_DIM), vec(HEAD_DIM), vec(HEAD_DIM), vec(LANES)],
        out_specs=pl.BlockSpec((tq, LANES), qmap),
        out_shape=jax.ShapeDtypeStruct(q.shape, BF16),
        compiler_params=_cparams(("parallel", "parallel", "arbitrary")),
        name="diff_attention",
    )(q, k, v, lq1.reshape(1, -1), lk1.reshape(1, -1), lq2.reshape(1, -1), lk2.reshape(1, -1),
      subln.reshape(1, -1))


def _sb_block(c, row0, q_ref, k_ref, v_ref, o_ref):
    tq = tk = SB_TQ
    nk = (c + 1) * tq
    z = _nt_dot(_stack_heads(q_ref[row0:row0 + tq, :]), k_ref[0:nk, :])
    log1m = -jnp.maximum(z, 0.0) - jnp.log(1.0 + jnp.exp(-jnp.abs(z)))
    log1m = _fill_diag(log1m, tq, True, 0.0)
    ri = lax.broadcasted_iota(I32, (2 * tk, tk), 0) & (tk - 1)
    ci = lax.broadcasted_iota(I32, (2 * tk, tk), 1)
    later = jnp.where(ri > ci, 1.0, 0.0).astype(BF16)
    run = jnp.zeros((2 * tq, 1), F32)
    acc = jnp.zeros((2 * tq, LANES), F32)
    for blk in reversed(range(nk // tk)):
        lb = log1m[:, blk * tk:(blk + 1) * tk]
        hi = lb.astype(BF16)
        lo = (lb - hi.astype(F32)).astype(BF16)
        local = jnp.dot(jnp.concatenate([hi, lo], axis=1), later, preferred_element_type=F32)
        a = jnp.exp(z[:, blk * tk:(blk + 1) * tk] + lb + (local + run))
        if blk == c:
            a = jnp.where(_diag_mask(2 * tq, tq, True), a, 0.0)
        acc = acc + jnp.dot(a.astype(BF16), v_ref[blk * tk:(blk + 1) * tk, :], preferred_element_type=F32)
        run = run + (local[:, 0:1] + lb[:, 0:1])
    o_ref[row0:row0 + tq, :] = _unstack_heads(acc).astype(o_ref.dtype)


def _sb_body(*refs):
    for i in range(refs[1].shape[0] // SB_TQ):
        _sb_block(i, i * SB_TQ, *refs)


def _sb_attention(q, k, v, batch, seq):
    pairs = q.shape[1] // LANES
    whole = lambda b, p: (b, p)
    return pl.pallas_call(
        _sb_body,
        grid=(batch, pairs),
        in_specs=[pl.BlockSpec((seq, LANES), whole),
                  pl.BlockSpec((seq, LANES), whole),
                  pl.BlockSpec((seq, LANES), whole)],
        out_specs=pl.BlockSpec((seq, LANES), whole),
        out_shape=jax.ShapeDtypeStruct(q.shape, BF16),
        compiler_params=_cparams(("parallel", "parallel")),
        name="stickbreak_attention",
    )(q, k, v)


def _dsa_body(top_k, iq_ref, iwt_ref, cq_ref, ika_ref, ikb_ref, kv_ref, ckvt_ref, mq_ref, wuvt_ref,
              o_ref, key_scr, sel_scr, qcat_scr, cut_scr, m_scr, l_scr, acc_scr, s_scr):
    tq, kc = DSA_TQ, DSA_KC
    qi = pl.program_id(1)
    n_chunks = qi + 1
    int_min = jnp.int32(INT_MIN)
    qpos = qi * tq + lax.broadcasted_iota(I32, (kc, tq), 1)
    krow = lax.broadcasted_iota(I32, (kc, tq), 0)
    group = DSA_COUNT_GROUP * kc
    grow = lax.broadcasted_iota(I32, (group, tq), 0)

    def over_chunks(fn, init):
        return lax.fori_loop(0, n_chunks, fn, init)

    def key_rows(j):
        return j // DSA_COUNT_GROUP, pl.ds(pl.multiple_of((j % DSA_COUNT_GROUP) * kc, kc), kc)

    def count_keys(pred_of_group):
        def add(g, acc):
            ones = jnp.where(pred_of_group(g * group, key_scr[g]), 1.0, 0.0)
            return acc + _fold_rows(ones, jnp.add)
        acc = lax.fori_loop(0, qi // DSA_COUNT_GROUP + 1, add, jnp.zeros((8, tq), F32))
        return jnp.sum(acc, axis=0, keepdims=True)

    iq = iq_ref[...]
    iq_st = jnp.concatenate([iq[:, :LANES], iq[:, LANES:]], axis=0)

    def build_keys(j, carry):
        k0 = pl.multiple_of(j * kc, kc)
        la = jnp.maximum(_nt_dot(ika_ref[pl.ds(k0, kc), :], iq_st), 0.0)
        lb = jnp.maximum(_nt_dot(ikb_ref[pl.ds(k0, kc), :], iq_st), 0.0)
        score = (la[:, :tq] * iwt_ref[0:1, :] + lb[:, :tq] * iwt_ref[1:2, :]
                 + la[:, tq:] * iwt_ref[2:3, :] + lb[:, tq:] * iwt_ref[3:4, :])
        bits = pltpu.bitcast(score, I32)
        keys = bits ^ ((bits >> 31) & jnp.int32(0x7FFFFFFF))
        keys = jnp.where(bits == int_min, 0, keys)
        g, rows = key_rows(j)
        key_scr[g, rows, :] = jnp.where(k0 + krow <= qpos, keys, int_min)
        return carry

    key_scr[qi // DSA_COUNT_GROUP] = jnp.full((group, tq), INT_MIN, I32)
    over_chunks(build_keys, 0)

    def value_bit(i, tu):
        cand = tu | lax.shift_left(jnp.int32(1), 31 - i)
        cnt = count_keys(lambda k0, keys: keys >= (cand ^ int_min))
        return jnp.where(cnt >= top_k, cand, tu)

    thr = lax.fori_loop(0, 32, value_bit, jnp.zeros((1, tq), I32)) ^ int_min
    need = top_k - count_keys(lambda k0, keys: keys > thr)

    excess = jnp.where(thr == int_min, 0.0, count_keys(lambda k0, keys: keys == thr) - need)
    index_bits = (kv_ref.shape[0] - 1).bit_length()
    cut_scr[...] = jnp.full(cut_scr.shape, 1 << index_bits, I32)

    @pl.when(jnp.max(excess) > 0.0)
    def _():
        def index_bit(i, x):
            cand = x | lax.shift_left(jnp.int32(1), index_bits - 1 - i)
            cnt = count_keys(lambda k0, keys: (keys == thr) & (k0 + grow < cand))
            return jnp.where(cnt < need, cand, x)

        cut = lax.fori_loop(0, index_bits, index_bit, jnp.zeros((1, tq), I32))
        cut_scr[...] = jnp.broadcast_to(cut, cut_scr.shape)

    cut = cut_scr[0:1, :]

    def build_mask(j, carry):
        g, rows = key_rows(j)
        keys = key_scr[g, rows, :]
        kpos = j * kc + krow
        chosen = ((keys > thr) | ((keys == thr) & (kpos <= cut))) & (kpos <= qpos)
        sel_scr[j] = jnp.where(chosen, 0.0, -jnp.inf)
        return carry

    over_chunks(build_mask, 0)

    for p in range(DSA_HEADS // 2):
        t = jnp.dot(cq_ref[p], mq_ref[p], preferred_element_type=F32)
        qcat_scr[(2 * p) * tq:(2 * p + 1) * tq, :] = t[:, :2 * LANES].astype(BF16)
        qcat_scr[(2 * p + 1) * tq:(2 * p + 2) * tq, :] = t[:, 2 * LANES:].astype(BF16)

    rows = DSA_HEADS * tq

    m_scr[...] = jnp.full(m_scr.shape, -jnp.inf, F32)
    l_scr[...] = jnp.zeros(l_scr.shape, F32)
    acc_scr[...] = jnp.zeros(acc_scr.shape, F32)

    def scores(j):
        k0 = pl.multiple_of(j * kc, kc)
        return _nt_dot(kv_ref[pl.ds(k0, kc), :], qcat_scr[...])

    s_scr[...] = scores(0)

    def attend(j, carry):
        s = s_scr[...]
        s_scr[...] = scores(jnp.minimum(j + 1, qi))
        s = s + jnp.concatenate([sel_scr[j]] * DSA_HEADS, axis=1)
        m = m_scr[...]
        m_new = jnp.maximum(m, jnp.max(_fold_rows(s, jnp.maximum), axis=0, keepdims=True))
        m_safe = jnp.where(m_new == -jnp.inf, 0.0, m_new)
        p = jnp.exp(s - m_safe)
        alpha = jnp.exp(m - m_safe)
        m_scr[...] = m_new
        l_scr[...] = alpha * l_scr[...] + jnp.sum(_fold_rows(p, jnp.add), axis=0, keepdims=True)
        acc_scr[...] = alpha * acc_scr[...] + jnp.dot(ckvt_ref[j], p.astype(BF16), preferred_element_type=F32)
        return carry

    over_chunks(attend, 0)
    o_lat = (acc_scr[...] / l_scr[...]).astype(BF16)
    for p in range(DSA_HEADS // 2):
        pair = jnp.concatenate([o_lat[:, (2 * p) * tq:(2 * p + 1) * tq],
                                o_lat[:, (2 * p + 1) * tq:(2 * p + 2) * tq]], axis=0)
        out_t = jnp.dot(wuvt_ref[p], pair, preferred_element_type=F32)
        o_ref[p] = out_t.T.astype(o_ref.dtype)


def _dsa_attention(iq, iwt, cq, ika, ikb, kv, ckvt, mq, wuvt, top_k, batch, seq):
    tq, kc = DSA_TQ, DSA_KC
    nq = seq // tq
    qmap = lambda b, i: (b * nq + i, 0)
    pmap = lambda b, i: (0, b * nq + i, 0)
    kmap = lambda b, i: (b, 0)
    cmap = lambda b, i: (0, 0, 0)
    pairs = cq.shape[0]
    return pl.pallas_call(
        functools.partial(_dsa_body, top_k),
        grid=(batch, nq),
        in_specs=[pl.BlockSpec((tq, iq.shape[1]), qmap),
                  pl.BlockSpec((IDX_HEADS, tq), lambda b, i: (0, b * nq + i)),
                  pl.BlockSpec((pairs, tq, LANES), pmap),
                  pl.BlockSpec((seq, LANES), kmap),
                  pl.BlockSpec((seq, LANES), kmap),
                  pl.BlockSpec((seq, 2 * LANES), kmap),
                  pl.BlockSpec((None, seq // kc, DSA_KV_RANK, kc), lambda b, i: (b, 0, 0, 0)),
                  pl.BlockSpec(mq.shape, cmap),
                  pl.BlockSpec(wuvt.shape, cmap)],
        out_specs=pl.BlockSpec((pairs, tq, LANES), pmap),
        out_shape=jax.ShapeDtypeStruct(cq.shape, BF16),
        scratch_shapes=[pltpu.VMEM((seq // (DSA_COUNT_GROUP * kc), DSA_COUNT_GROUP * kc, tq), I32),
                        pltpu.VMEM((seq // kc, kc, tq), F32),
                        pltpu.VMEM((DSA_HEADS * tq, 2 * LANES), BF16), pltpu.VMEM((8, tq), I32),
                        pltpu.VMEM((1, DSA_HEADS * tq), F32), pltpu.VMEM((1, DSA_HEADS * tq), F32),
                        pltpu.VMEM((DSA_KV_RANK, DSA_HEADS * tq), F32), pltpu.VMEM((kc, DSA_HEADS * tq), F32)],
        compiler_params=_cparams(("parallel", "arbitrary")),
        name="dsa_attention",
    )(iq, iwt, cq, ika, ikb, kv, ckvt, mq, wuvt)


def _rope_tables(positions):
    inv_freq = ROPE_THETA ** (-jnp.arange(0, ROT_DIM, 2, dtype=F32) / ROT_DIM)
    ang = positions.astype(F32).reshape(-1, 1) * inv_freq
    cos, sin = jnp.cos(ang), jnp.sin(ang)
    t = ang.shape[0]
    pad = jnp.zeros((t, HEAD_DIM - ROT_DIM), F32)
    zero8 = jnp.zeros((t, ROT_HALF), F32)
    c64 = jnp.concatenate([cos, cos, pad + 1.0], axis=1)
    sa64 = jnp.concatenate([-sin, zero8, pad], axis=1)
    sb64 = jnp.concatenate([zero8, sin, pad], axis=1)
    return tuple(jnp.tile(a, (1, LANES // HEAD_DIM)) for a in (c64, sa64, sb64))


def _pad_cols(w, width):
    return jnp.pad(w, ((0, 0), (0, width - w.shape[1])))


EVEN_GROUPS = ((FOX_WIDTH, "plain", False), (FOX_WIDTH, "plain", False), (FOX_WIDTH, "plain", False),
               (DIFF_WIDTH, "rope", False), (DIFF_WIDTH, "rope", False), (DIFF_WIDTH, "plain", False),
               (LANES, "f32", False))
ODD_GROUPS = ((DSA_WIDTH, "rope", True), (IDX_HEADS * IDX_DIM, "rope", False), (LANES, "rope", False),
              (LANES, "rope", False), (2 * LANES, "kv", False),
              (SB_WIDTH, "plain", False), (SB_WIDTH, "plain", False), (SB_WIDTH, "plain", False),
              (LANES, "f32", False))


def _pack_even(w):
    aq, ak, av, af, bq, bk, bv = jnp.split(w, (512, 1024, 1536, 1544, 2056, 2568), axis=1)
    return jnp.concatenate([aq * QK_SCALE, ak, av, bq * QK_SCALE, bk, bv, _pad_cols(af, LANES)],
                           axis=1).astype(BF16)


def _pack_odd(w):
    cq, ckv, ckr, ciq, ciw, cik, sq, sk, sv = jnp.split(w, (512, 640, 656, 912, 916, 980, 1492, 2004), axis=1)
    zeros64 = jnp.zeros_like(cik)
    return jnp.concatenate([cq * QK_SCALE, ciq,
                            jnp.concatenate([cik, zeros64], axis=1), jnp.concatenate([zeros64, cik], axis=1),
                            ckv, _pad_cols(ckr, LANES),
                            sq * QK_SCALE, sk, sv, _pad_cols(ciw * (IDX_HEADS ** -0.5), LANES)],
                           axis=1).astype(BF16)


def _pack_dsa_weights(w_uk, w_uv):
    mq, wuv = [], []
    eye = jnp.eye(ROT_DIM, dtype=F32)
    for p in range(DSA_HEADS // 2):
        m = jnp.zeros((LANES, 4 * LANES), F32)
        u = jnp.zeros((2 * DSA_KV_RANK, LANES), F32)
        for hh in range(2):
            h = 2 * p + hh
            r0 = hh * HEAD_DIM
            c0 = hh * 2 * LANES
            m = m.at[r0 + ROT_DIM:r0 + HEAD_DIM, c0:c0 + DSA_KV_RANK].set(w_uk[:, h, :].T)
            m = m.at[r0:r0 + ROT_DIM, c0 + DSA_KV_RANK:c0 + DSA_KV_RANK + ROT_DIM].set(eye)
            u = u.at[hh * DSA_KV_RANK:(hh + 1) * DSA_KV_RANK, r0:r0 + HEAD_DIM].set(w_uv[:, h, :])
        mq.append(m)
        wuv.append(u)
    return jnp.stack(mq).astype(BF16), jnp.stack(wuv).astype(BF16)


def kernel(x, positions, attn_norm, ffn_norm, final_norm, ev_w_in, ev_fgate_b, ev_lambda_q1, ev_lambda_k1,
           ev_lambda_q2, ev_lambda_k2, ev_subln, ev_w_out, od_w_in, od_kv_norm, od_w_uk, od_w_uv, od_w_out,
           ffn_w_in, ffn_conv_w, ffn_conv_b, ffn_w_out):
    batch, seq, _ = x.shape
    depth = attn_norm.shape[0]
    t = batch * seq
    top_k = min(IDX_TOPK_MAX, seq // 4)
    tabs = _rope_tables(positions)
    xf = x.reshape(t, D_MODEL)
    ones_kv = jnp.ones((LANES,), F32)
    for layer in range(depth):
        j = layer // 2
        if layer % 2 == 0:
            aq, ak, av, bq, bk, bv, af = _in_proj(xf, attn_norm[layer], _pack_even(ev_w_in[j]), tabs, ones_kv,
                                                  EVEN_GROUPS)
            c = _fgate_cumsum(af, _pad_cols(ev_fgate_b[j].reshape(1, -1), LANES), batch, seq)
            c8 = c.reshape(batch, seq, LANES)[:, :, :FOX_WIDTH // HEAD_DIM]
            ccol = c8.reshape(batch, seq, -1, 2).transpose(0, 2, 1, 3)
            crow = c8.reshape(batch, seq, -1, 2).transpose(0, 2, 3, 1)
            o_a = _fox_attention(aq, ak, av, ccol, crow, batch, seq)
            lam_init = 0.8 - 0.6 * math.exp(-0.3 * layer)
            o_b = _diff_attention(bq, bk, bv, ev_lambda_q1[j], ev_lambda_k1[j], ev_lambda_q2[j], ev_lambda_k2[j],
                                  ev_subln[j], lam_init, batch, seq)
            xf = _out_proj(xf, o_a, o_b, ev_w_out[j].astype(BF16))
        else:
            cq, iq, ika, ikb, kv, sq, sk, sv, iw = _in_proj(xf, attn_norm[layer], _pack_odd(od_w_in[j]), tabs,
                                                            od_kv_norm[j], ODD_GROUPS)
            mq, wuv = _pack_dsa_weights(od_w_uk[j], od_w_uv[j])
            ckvt = kv[:, :DSA_KV_RANK].reshape(batch, seq // DSA_KC, DSA_KC, DSA_KV_RANK).transpose(0, 1, 3, 2)
            o_c = _dsa_attention(iq, iw[:, :IDX_HEADS].T, cq, ika, ikb, kv, ckvt, mq, wuv.transpose(0, 2, 1),
                                 top_k, batch, seq)
            o_d = _sb_attention(sq, sk, sv, batch, seq)
            xf = _out_proj(xf, o_c, o_d, od_w_out[j].astype(BF16))
        xf = _ffn(xf, ffn_norm[layer], ffn_w_in[layer].astype(BF16), ffn_conv_w[layer], ffn_conv_b[layer],
                  ffn_w_out[layer].astype(BF16), final_norm, seq, layer == depth - 1)
    return xf.reshape(batch, seq, D_MODEL)
```
